```python
import math
import jax, jax.numpy as jnp
from jax import lax
import numpy as np

D_MODEL = 1024
BATCH = 8
SEQ = 4096
DEPTH = 2

N_A_LAYERS = DEPTH // 2
N_B_LAYERS = DEPTH - N_A_LAYERS
RMS_EPS = 1e-6
GATED_NORM_EPS = 1e-5

SSM_EXPAND = 2
SSM_D_INNER = SSM_EXPAND * D_MODEL
SSM_HEAD_DIM = 64
SSM_N_HEADS = SSM_D_INNER // SSM_HEAD_DIM
SSM_N_GROUPS = 8
SSM_D_STATE = 128
SSM_CONV = 4
SSM_CHUNK = 128
SSM_CONV_DIM = SSM_D_INNER + 2 * SSM_N_GROUPS * SSM_D_STATE
SSM_IN_DIM = SSM_D_INNER + SSM_CONV_DIM + SSM_N_HEADS

ATT_PATTERNS = ((128, 1), (512, 4), (2048, 16))
ATT_N_GROUPS = len(ATT_PATTERNS)
ATT_HEAD_DIM = 128
ATT_HEADS_PER_GROUP = 8
ATT_KV_HEADS_PER_GROUP = 2
ATT_Q_DIM = ATT_N_GROUPS * ATT_HEADS_PER_GROUP * ATT_HEAD_DIM
ATT_KV_DIM = ATT_N_GROUPS * ATT_KV_HEADS_PER_GROUP * ATT_HEAD_DIM
ATT_OUT_DIM = ATT_HEADS_PER_GROUP * ATT_HEAD_DIM
ROPE_DIM = ATT_HEAD_DIM // 4
ROPE_THETA = 500000.0

FFN_DIM = 2816
FFN_CONV = 3

kernel_name = 'yoco_mamba2_dilated_attn_hybrid'


def _rms_norm(x, w, eps=RMS_EPS):
    xf = x.astype(jnp.float32)
    y = xf * lax.rsqrt(jnp.mean(xf * xf, axis=-1, keepdims=True) + eps)
    return (y * w.astype(jnp.float32)).astype(x.dtype)


def _causal_depthwise_conv(x, w):
    width, ch = w.shape
    return lax.conv_general_dilated(
        x, w[:, None, :].astype(x.dtype), window_strides=(1,),
        padding=((width - 1, 0),), dimension_numbers=('NWC', 'WIO', 'NWC'),
        feature_group_count=ch)


def _partial_rotary(x, positions):
    half = ROPE_DIM // 2
    inv_freq = jnp.power(jnp.float32(ROPE_THETA), -jnp.arange(0, ROPE_DIM, 2, dtype=jnp.float32) / ROPE_DIM)
    ang = positions.astype(jnp.float32)[:, None] * inv_freq[None, :]
    cos = jnp.cos(ang)[None, :, None, :]
    sin = jnp.sin(ang)[None, :, None, :]
    xf = x.astype(jnp.float32)
    x1, x2, rest = xf[..., :half], xf[..., half:ROPE_DIM], xf[..., ROPE_DIM:]
    out = jnp.concatenate([x1 * cos - x2 * sin, x2 * cos + x1 * sin, rest], axis=-1)
    return out.astype(x.dtype)


def _ssd_chunked(x, dt, a, b_in, c_in):
    bsz, seq, nh, hp = x.shape
    ng, ns = b_in.shape[2], b_in.shape[3]
    hg = nh // ng
    nc, L = seq // SSM_CHUNK, SSM_CHUNK
    xc = (x.astype(jnp.float32) * dt[..., None]).reshape(bsz, nc, L, ng, hg, hp)
    adt = (dt * a).reshape(bsz, nc, L, ng, hg).transpose(0, 1, 3, 4, 2)
    a_cs = jnp.cumsum(adt, axis=-1)
    bc = b_in.astype(jnp.float32).reshape(bsz, nc, L, ng, ns)
    cc = c_in.astype(jnp.float32).reshape(bsz, nc, L, ng, ns)
    tril = jnp.tril(jnp.ones((L, L), dtype=bool))
    seg = a_cs[..., :, None] - a_cs[..., None, :]
    decay = jnp.exp(jnp.where(tril, seg, -jnp.inf))
    cb = jnp.einsum('bclgn,bcsgn->bcgls', cc, bc)
    y_diag = jnp.einsum('bcgls,bcghls,bcsghp->bclghp', cb, decay, xc)
    decay_states = jnp.exp(a_cs[..., -1:] - a_cs)
    states = jnp.einsum('bclgn,bcghl,bclghp->bcghpn', bc, decay_states, xc)
    chunk_decay = jnp.exp(a_cs[..., -1])

    def step(h, inp):
        st, dec = inp
        return h * dec[..., None, None] + st, h

    h0 = jnp.zeros((bsz, ng, hg, hp, ns), jnp.float32)
    _, prev = lax.scan(step, h0, (jnp.moveaxis(states, 1, 0), jnp.moveaxis(chunk_decay, 1, 0)))
    prev = jnp.moveaxis(prev, 0, 1)
    y_off = jnp.einsum('bclgn,bcghpn,bcghl->bclghp', cc, prev, jnp.exp(a_cs))
    return (y_diag + y_off).reshape(bsz, seq, nh, hp)


def _mamba2_mixer(h, w_in, conv_w, conv_b, dt_bias, a_log, d_skip, norm_w, w_out):
    bsz, seq, _ = h.shape
    zxbcdt = h @ w_in.astype(h.dtype)
    z = zxbcdt[..., :SSM_D_INNER]
    xbc = zxbcdt[..., SSM_D_INNER:SSM_D_INNER + SSM_CONV_DIM]
    dt_raw = zxbcdt[..., SSM_D_INNER + SSM_CONV_DIM:]
    xbc = jax.nn.silu(_causal_depthwise_conv(xbc, conv_w) + conv_b.astype(h.dtype))
    gn = SSM_N_GROUPS * SSM_D_STATE
    xs = xbc[..., :SSM_D_INNER].reshape(bsz, seq, SSM_N_HEADS, SSM_HEAD_DIM)
    b_in = xbc[..., SSM_D_INNER:SSM_D_INNER + gn].reshape(bsz, seq, SSM_N_GROUPS, SSM_D_STATE)
    c_in = xbc[..., SSM_D_INNER + gn:].reshape(bsz, seq, SSM_N_GROUPS, SSM_D_STATE)
    dt = jax.nn.softplus(dt_raw.astype(jnp.float32) + dt_bias.astype(jnp.float32))
    a = -jnp.exp(a_log.astype(jnp.float32))
    y = _ssd_chunked(xs, dt, a, b_in, c_in)
    y = y + xs.astype(jnp.float32) * d_skip.astype(jnp.float32)[:, None]
    y = y.reshape(bsz, seq, SSM_D_INNER) * jax.nn.silu(z.astype(jnp.float32))
    yg = y.reshape(bsz, seq, SSM_N_GROUPS, SSM_D_INNER // SSM_N_GROUPS)
    yg = yg * lax.rsqrt(jnp.mean(yg * yg, axis=-1, keepdims=True) + GATED_NORM_EPS)
    y = yg.reshape(bsz, seq, SSM_D_INNER) * norm_w.astype(jnp.float32)
    return y.astype(h.dtype) @ w_out.astype(h.dtype)


def _shared_kv(x, kv_norm, w_kv, positions):
    bsz, seq, _ = x.shape
    kv = (_rms_norm(x, kv_norm) @ w_kv.astype(x.dtype)).reshape(
        bsz, seq, 2, ATT_N_GROUPS * ATT_KV_HEADS_PER_GROUP, ATT_HEAD_DIM)
    k = _partial_rotary(kv[:, :, 0], positions).reshape(bsz, seq, ATT_N_GROUPS, ATT_KV_HEADS_PER_GROUP, ATT_HEAD_DIM)
    v = kv[:, :, 1].reshape(bsz, seq, ATT_N_GROUPS, ATT_KV_HEADS_PER_GROUP, ATT_HEAD_DIM)
    return k, v


def _dilated_group_attention(q, k, v, window, dilation):
    bsz, seq, n_heads, hd = q.shape
    n_kv = k.shape[2]
    rep = n_heads // n_kv
    blk = window // dilation
    n_sub = seq // dilation
    n_blk = -(-n_sub // blk)
    n_pad = n_blk * blk

    def to_blocks(t):
        t = jnp.moveaxis(t.reshape(bsz, n_sub, dilation, *t.shape[2:]), 2, 1)
        pad = [(0, 0)] * t.ndim
        pad[2] = (0, n_pad - n_sub)
        t = jnp.pad(t, pad)
        return t.reshape(bsz, dilation, n_blk, blk, *t.shape[3:])

    def with_prev(t):
        prev = jnp.pad(t[:, :, :-1], ((0, 0), (0, 0), (1, 0), (0, 0), (0, 0), (0, 0)))
        return jnp.concatenate([prev, t], axis=3)

    qb = to_blocks(q.astype(jnp.float32)).reshape(bsz, dilation, n_blk, blk, n_kv, rep, hd)
    kb = with_prev(to_blocks(k.astype(jnp.float32)))
    vb = with_prev(to_blocks(v.astype(jnp.float32)))
    scores = jnp.einsum('brnqhgd,brnshd->brnhgqs', qb, kb) * (hd ** -0.5)
    q_idx = jnp.arange(blk)[:, None]
    s_idx = jnp.arange(2 * blk)[None, :]
    rel = q_idx + blk - s_idx
    band = (rel >= 0) & (rel <= blk)
    has_prev = (jnp.arange(n_blk) > 0)[:, None, None] | (s_idx >= blk)[None]
    mask = band[None] & has_prev
    scores = jnp.where(mask[:, None, None], scores, -jnp.inf)
    m = jnp.max(scores, axis=-1, keepdims=True)
    p = jnp.exp(scores - m)
    denom = jnp.sum(p, axis=-1)
    lse = jnp.transpose(m[..., 0] + jnp.log(denom), (0, 1, 2, 5, 3, 4))
    out = jnp.einsum('brnhgqs,brnshd->brnqhgd', p, vb) / jnp.transpose(denom, (0, 1, 2, 5, 3, 4))[..., None]

    def from_blocks(t):
        t = t.reshape(bsz, dilation, n_pad, n_heads, *t.shape[6:])[:, :, :n_sub]
        return jnp.moveaxis(t, 1, 2).reshape(bsz, seq, n_heads, *t.shape[4:])

    return from_blocks(out), from_blocks(lse)


def _dilated_mixer(h, k_sh, v_sh, w_q, w_o, positions):
    bsz, seq, _ = h.shape
    q = (h @ w_q.astype(h.dtype)).reshape(bsz, seq, ATT_N_GROUPS * ATT_HEADS_PER_GROUP, ATT_HEAD_DIM)
    q = _partial_rotary(q, positions).reshape(bsz, seq, ATT_N_GROUPS, ATT_HEADS_PER_GROUP, ATT_HEAD_DIM)
    outs, lses = [], []
    for g, (window, dilation) in enumerate(ATT_PATTERNS):
        o_g, lse_g = _dilated_group_attention(q[:, :, g], k_sh[:, :, g], v_sh[:, :, g], window, dilation)
        outs.append(o_g)
        lses.append(lse_g)
    wts = jax.nn.softmax(jnp.stack(lses, axis=0), axis=0)
    o = jnp.einsum('gbsh,gbshd->bshd', wts, jnp.stack(outs, axis=0))
    return o.reshape(bsz, seq, ATT_OUT_DIM).astype(h.dtype) @ w_o.astype(h.dtype)


def _conv_ffn(h, w_up, conv_w, w_down):
    u = _causal_depthwise_conv(h @ w_up.astype(h.dtype), conv_w)
    gate, val = u[..., :FFN_DIM], u[..., FFN_DIM:]
    return (jax.nn.silu(gate) * val) @ w_down.astype(h.dtype)


def setup_inputs(seed: int = 0) -> dict:
    key = jax.random.key(seed)
    ks = jax.random.split(key, 24)
    f32 = jnp.float32
    out_scale = (2.0 * DEPTH) ** -0.5

    def nrm(k, shape, scale):
        return jax.random.normal(k, shape, f32) * scale

    x = nrm(ks[0], (BATCH, SEQ, D_MODEL), 1.0)
    a_norm = 1.0 + nrm(ks[1], (N_A_LAYERS, D_MODEL), 0.02)
    ssm_w_in = nrm(ks[2], (N_A_LAYERS, D_MODEL, SSM_IN_DIM), D_MODEL ** -0.5)
    ssm_conv_w = nrm(ks[3], (N_A_LAYERS, SSM_CONV, SSM_CONV_DIM), SSM_CONV ** -0.5)
    ssm_conv_b = nrm(ks[4], (N_A_LAYERS, SSM_CONV_DIM), 0.02)
    dt0 = jnp.exp(jax.random.uniform(ks[5], (N_A_LAYERS, SSM_N_HEADS), f32, math.log(1e-3), math.log(1e-1)))
    ssm_dt_bias = dt0 + jnp.log(-jnp.expm1(-dt0))
    ssm_a_log = jnp.log(jax.random.uniform(ks[6], (N_A_LAYERS, SSM_N_HEADS), f32, 1.0, 16.0))
    ssm_d = 1.0 + nrm(ks[7], (N_A_LAYERS, SSM_N_HEADS), 0.02)
    ssm_norm = 1.0 + nrm(ks[8], (N_A_LAYERS, SSM_D_INNER), 0.02)
    ssm_w_out = nrm(ks[9], (N_A_LAYERS, SSM_D_INNER, D_MODEL), SSM_D_INNER ** -0.5 * out_scale)
    kv_norm = 1.0 + nrm(ks[10], (D_MODEL,), 0.02)
    w_kv = nrm(ks[11], (D_MODEL, 2 * ATT_KV_DIM), D_MODEL ** -0.5)
    b_norm = 1.0 + nrm(ks[12], (N_B_LAYERS, D_MODEL), 0.02)
    att_w_q = nrm(ks[13], (N_B_LAYERS, D_MODEL, ATT_Q_DIM), D_MODEL ** -0.5)
    att_w_o = nrm(ks[14], (N_B_LAYERS, ATT_OUT_DIM, D_MODEL), ATT_OUT_DIM ** -0.5 * out_scale)
    ffn_norm = 1.0 + nrm(ks[15], (DEPTH, D_MODEL), 0.02)
    ffn_w_up = nrm(ks[16], (DEPTH, D_MODEL, 2 * FFN_DIM), D_MODEL ** -0.5)
    ffn_conv_w = nrm(ks[17], (DEPTH, FFN_CONV, 2 * FFN_DIM), FFN_CONV ** -0.5)
    ffn_w_down = nrm(ks[18], (DEPTH, FFN_DIM, D_MODEL), FFN_DIM ** -0.5 * out_scale)
    final_norm = 1.0 + nrm(ks[19], (D_MODEL,), 0.02)
    return {'x': x, 'a_norm': a_norm, 'ssm_w_in': ssm_w_in, 'ssm_conv_w': ssm_conv_w,
            'ssm_conv_b': ssm_conv_b, 'ssm_dt_bias': ssm_dt_bias, 'ssm_a_log': ssm_a_log,
            'ssm_d': ssm_d, 'ssm_norm': ssm_norm, 'ssm_w_out': ssm_w_out, 'kv_norm': kv_norm,
            'w_kv': w_kv, 'b_norm': b_norm, 'att_w_q': att_w_q, 'att_w_o': att_w_o,
            'ffn_norm': ffn_norm, 'ffn_w_up': ffn_w_up, 'ffn_conv_w': ffn_conv_w,
            'ffn_w_down': ffn_w_down, 'final_norm': final_norm}


def reference(x, a_norm, ssm_w_in, ssm_conv_w, ssm_conv_b, ssm_dt_bias, ssm_a_log, ssm_d,
              ssm_norm, ssm_w_out, kv_norm, w_kv, b_norm, att_w_q, att_w_o,
              ffn_norm, ffn_w_up, ffn_conv_w, ffn_w_down, final_norm):
    positions = jnp.arange(x.shape[1], dtype=jnp.int32)
    k_sh, v_sh = None, None
    for layer in range(DEPTH):
        if layer < N_A_LAYERS:
            i = layer
            x = x + _mamba2_mixer(_rms_norm(x, a_norm[i]), ssm_w_in[i], ssm_conv_w[i], ssm_conv_b[i],
                                  ssm_dt_bias[i], ssm_a_log[i], ssm_d[i], ssm_norm[i], ssm_w_out[i])
        else:
            if layer == N_A_LAYERS:
                k_sh, v_sh = _shared_kv(x, kv_norm, w_kv, positions)
            j = layer - N_A_LAYERS
            x = x + _dilated_mixer(_rms_norm(x, b_norm[j]), k_sh, v_sh, att_w_q[j], att_w_o[j], positions)
        x = x + _conv_ffn(_rms_norm(x, ffn_norm[layer]), ffn_w_up[layer], ffn_conv_w[layer], ffn_w_down[layer])
    return _rms_norm(x, final_norm)
```

```python
import functools

import numpy as np
import jax
import jax.numpy as jnp
from jax import lax
from jax.experimental import pallas as pl
from jax.experimental.pallas import tpu as pltpu

F32 = jnp.float32
BF16 = jnp.bfloat16

D_MODEL = 1024
RMS_EPS = 1e-6
GATED_NORM_EPS = 1e-5

SSM_D_INNER = 2048
SSM_HEAD_DIM = 64
SSM_N_HEADS = 32
SSM_N_GROUPS = 8
SSM_HEADS_PER_GROUP = SSM_N_HEADS // SSM_N_GROUPS
SSM_GROUP_CH = SSM_D_INNER // SSM_N_GROUPS
SSM_D_STATE = 128
SSM_CONV = 4
SSM_CHUNK = 128
SSM_BC_DIM = SSM_N_GROUPS * SSM_D_STATE
SSM_CONV_DIM = SSM_D_INNER + 2 * SSM_BC_DIM

ATT_DILATIONS = (1, 4, 16)
ATT_BLOCK = 128
ATT_HEAD_DIM = 128
ATT_HEADS = 8
ATT_KV_HEADS = 2
ATT_REP = ATT_HEADS // ATT_KV_HEADS
ATT_TILE = ATT_BLOCK * max(ATT_DILATIONS)
ATT_Q_GROUP_DIM = ATT_HEADS * ATT_HEAD_DIM
ATT_KV_GROUP_DIM = ATT_KV_HEADS * ATT_HEAD_DIM
ROPE_DIM = 32
ROPE_THETA = 500000.0

FFN_DIM = 2816
FFN_CONV = 3
FFN_COL = 256

LANES = 128
SUBLANES = 8
NEG_BIG = -1e30
VMEM_LIMIT = 56 * 1024 * 1024

HIGHEST = lax.Precision.HIGHEST


def _const_spec(shape):
    nd = len(shape)
    return pl.BlockSpec(shape, lambda *_: (0,) * nd, pipeline_mode=pl.Buffered(1))


def _params(sem):
    return pltpu.CompilerParams(dimension_semantics=sem, vmem_limit_bytes=VMEM_LIMIT)


def _silu(v):
    return v / (1.0 + jnp.exp(-v))


def _rms_normed(x, w):
    return x * lax.rsqrt(jnp.mean(x * x, axis=-1, keepdims=True) + RMS_EPS) * w


IN_TM = 512
IN_NC = 512


def _in_proj_kernel(x_ref, nw_ref, wz_ref, wx_ref, wdt_ref, z_ref, xbc_ref, dt_ref):
    h = _rms_normed(x_ref[...], nw_ref[...])
    hb = h.astype(BF16)
    for c in range(0, SSM_D_INNER, IN_NC):
        z_ref[:, c:c + IN_NC] = jnp.dot(hb, wz_ref[:, c:c + IN_NC], preferred_element_type=F32).astype(BF16)
    for c in range(0, SSM_CONV_DIM, IN_NC):
        xbc_ref[:, c:c + IN_NC] = jnp.dot(hb, wx_ref[:, c:c + IN_NC], preferred_element_type=F32).astype(BF16)
    dt_ref[...] = jnp.dot(h, wdt_ref[...], preferred_element_type=F32, precision=HIGHEST)


def _in_proj(x2, nw, wz, wx, wdt):
    t = x2.shape[0]
    return pl.pallas_call(
        _in_proj_kernel,
        grid=(t // IN_TM,),
        in_specs=[
            pl.BlockSpec((IN_TM, D_MODEL), lambda i: (i, 0)),
            _const_spec((1, D_MODEL)),
            _const_spec((D_MODEL, SSM_D_INNER)),
            _const_spec((D_MODEL, SSM_CONV_DIM)),
            _const_spec((D_MODEL, LANES)),
        ],
        out_specs=[
            pl.BlockSpec((IN_TM, SSM_D_INNER), lambda i: (i, 0)),
            pl.BlockSpec((IN_TM, SSM_CONV_DIM), lambda i: (i, 0)),
            pl.BlockSpec((IN_TM, LANES), lambda i: (i, 0)),
        ],
        out_shape=[
            jax.ShapeDtypeStruct((t, SSM_D_INNER), BF16),
            jax.ShapeDtypeStruct((t, SSM_CONV_DIM), BF16),
            jax.ShapeDtypeStruct((t, LANES), F32),
        ],
        compiler_params=_params(("parallel",)),
        name="in_proj",
    )(x2, nw, wz, wx, wdt)


SSD_CONV_COLS = 512


def _ssd_kernel(xbc_ref, z_ref, dt_ref, cw_ref, cb_ref, dtb_ref, alog_ref, dsk_ref, nw_ref,
                y_ref, ext_ref, conv_ref, state_ref):
    L = SSM_CHUNK
    c_idx = pl.program_id(1)

    @pl.when(c_idx == 0)
    def _():
        ext_ref[0:SUBLANES, :] = jnp.zeros((SUBLANES, SSM_CONV_DIM), F32)
        state_ref[...] = jnp.zeros_like(state_ref)

    ext_ref[SUBLANES:SUBLANES + L, :] = xbc_ref[...].astype(F32)
    for c in range(0, SSM_CONV_DIM, SSD_CONV_COLS):
        cols = slice(c, c + SSD_CONV_COLS)
        acc = jnp.broadcast_to(cb_ref[:, cols], (L, SSD_CONV_COLS))
        for k in range(SSM_CONV):
            off = SUBLANES - (SSM_CONV - 1) + k
            acc = acc + cw_ref[k:k + 1, cols] * ext_ref[off:off + L, cols]
        conv_ref[:, cols] = _silu(acc)
    ext_ref[0:SUBLANES, :] = ext_ref[L:L + SUBLANES, :]

    dt_raw = dt_ref[...] + dtb_ref[...]
    dt = jnp.maximum(dt_raw, 0.0) + jnp.log1p(jnp.exp(-jnp.abs(dt_raw)))
    adt = dt * (-jnp.exp(alog_ref[...]))
    row = lax.broadcasted_iota(jnp.int32, (L, L), 0)
    col = lax.broadcasted_iota(jnp.int32, (L, L), 1)
    causal = row >= col
    tril = causal.astype(F32)
    a_cs = jnp.dot(tril, adt, preferred_element_type=F32, precision=HIGHEST)
    a_cs_t = jnp.dot(adt.T, tril.T, preferred_element_type=F32, precision=HIGHEST)
    a_last = a_cs[L - 1:L, :]
    e_acs = jnp.exp(a_cs)
    w_state = dt * jnp.exp(a_last - a_cs)
    e_last = jnp.exp(a_last)

    for g in range(SSM_N_GROUPS):
        xcols = slice(g * SSM_GROUP_CH, (g + 1) * SSM_GROUP_CH)
        b_g = conv_ref[:, SSM_D_INNER + g * SSM_D_STATE:SSM_D_INNER + (g + 1) * SSM_D_STATE]
        c_g = conv_ref[:, SSM_D_INNER + SSM_BC_DIM + g * SSM_D_STATE:
                       SSM_D_INNER + SSM_BC_DIM + (g + 1) * SSM_D_STATE]
        b_bf = b_g.astype(BF16)
        c_bf = c_g.astype(BF16)
        cb = lax.dot_general(c_bf, b_bf, (((1,), (1,)), ((), ())), preferred_element_type=F32)
        prev = state_ref[g]
        y_off = jnp.dot(c_bf, prev.astype(BF16), preferred_element_type=F32)

        ys, xws, decs = [], [], []
        for hh in range(SSM_HEADS_PER_GROUP):
            h = g * SSM_HEADS_PER_GROUP + hh
            x_h = conv_ref[:, h * SSM_HEAD_DIM:(h + 1) * SSM_HEAD_DIM]
            seg = a_cs[:, h:h + 1] - a_cs_t[h:h + 1, :]
            decay = jnp.exp(jnp.where(causal, seg, NEG_BIG))
            m_h = (cb * decay).astype(BF16)
            xdt = x_h * dt[:, h:h + 1]
            y_h = jnp.dot(m_h, xdt.astype(BF16), preferred_element_type=F32)
            y_h = y_h + y_off[:, hh * SSM_HEAD_DIM:(hh + 1) * SSM_HEAD_DIM] * e_acs[:, h:h + 1]
            y_h = y_h + x_h * dsk_ref[:, h * SSM_HEAD_DIM:(h + 1) * SSM_HEAD_DIM]
            ys.append(y_h)
            xws.append(x_h * w_state[:, h:h + 1])
            decs.append(jnp.broadcast_to(e_last[:, h:h + 1], (1, SSM_HEAD_DIM)))
        xw = jnp.concatenate(xws, axis=1).astype(BF16)
        new_state = jnp.dot(b_bf.T, xw, preferred_element_type=F32)
        state_ref[g] = prev * jnp.concatenate(decs, axis=1) + new_state

        y_g = jnp.concatenate(ys, axis=1) * _silu(z_ref[:, xcols].astype(F32))
        y_g = y_g * lax.rsqrt(jnp.mean(y_g * y_g, axis=-1, keepdims=True) + GATED_NORM_EPS)
        y_ref[:, xcols] = (y_g * nw_ref[:, xcols]).astype(BF16)


def _ssd(xbc, z, dt, cw, cb, dtb, alog, dsk, nw, bsz, seq):
    L = SSM_CHUNK
    nc = seq // L
    rows = lambda b, c: (b * nc + c, 0)
    return pl.pallas_call(
        _ssd_kernel,
        grid=(bsz, nc),
        in_specs=[
            pl.BlockSpec((L, SSM_CONV_DIM), rows),
            pl.BlockSpec((L, SSM_D_INNER), rows),
            pl.BlockSpec((L, LANES), rows),
            _const_spec((SSM_CONV, SSM_CONV_DIM)),
            _const_spec((1, SSM_CONV_DIM)),
            _const_spec((1, LANES)),
            _const_spec((1, LANES)),
            _const_spec((1, SSM_D_INNER)),
            _const_spec((1, SSM_D_INNER)),
        ],
        out_specs=pl.BlockSpec((L, SSM_D_INNER), rows),
        out_shape=jax.ShapeDtypeStruct((bsz * seq, SSM_D_INNER), BF16),
        scratch_shapes=[
            pltpu.VMEM((SUBLANES + L, SSM_CONV_DIM), F32),
            pltpu.VMEM((L, SSM_CONV_DIM), F32),
            pltpu.VMEM((SSM_N_GROUPS, SSM_D_STATE, SSM_GROUP_CH), F32),
        ],
        compiler_params=_params(("parallel", "arbitrary")),
        name="ssd",
    )(xbc, z, dt, cw, cb, dtb, alog, dsk, nw)


FFN_TM = 512


def _ffn_kernel(x_ref, pre_ref, wpre_ref, nw_ref, wup_ref, cw_ref, wdn_ref, fnw_ref, o_ref,
                ext_ref, tail_ref, *, final_norm):
    tm = FFN_TM

    @pl.when(pl.program_id(1) == 0)
    def _():
        tail_ref[...] = jnp.zeros_like(tail_ref)

    x1 = x_ref[...] + jnp.dot(pre_ref[...], wpre_ref[...], preferred_element_type=F32)
    hb = _rms_normed(x1, nw_ref[...]).astype(BF16)
    acc = x1
    for c in range(FFN_DIM // FFN_COL):
        halves = []
        for half in range(2):
            off = half * FFN_DIM + c * FFN_COL
            cols = slice(off, off + FFN_COL)
            u = jnp.dot(hb, wup_ref[:, cols], preferred_element_type=F32)
            ext_ref[half, 0:SUBLANES, :] = tail_ref[:, cols]
            ext_ref[half, SUBLANES:SUBLANES + tm, :] = u
            tail_ref[:, cols] = u[tm - SUBLANES:tm, :]
            v = cw_ref[FFN_CONV - 1:FFN_CONV, cols] * u
            for k in range(FFN_CONV - 1):
                o = SUBLANES - (FFN_CONV - 1) + k
                v = v + cw_ref[k:k + 1, cols] * ext_ref[half, o:o + tm, :]
            halves.append(v)
        act = (_silu(halves[0]) * halves[1]).astype(BF16)
        acc = acc + jnp.dot(act, wdn_ref[c * FFN_COL:(c + 1) * FFN_COL, :], preferred_element_type=F32)
    if final_norm:
        acc = _rms_normed(acc, fnw_ref[...])
    o_ref[...] = acc


def _ffn(x2, pre, wpre, nw, wup, cw, wdn, fnw, bsz, seq, final_norm):
    nt = seq // FFN_TM
    kp = pre.shape[1]
    rows = lambda b, s: (b * nt + s, 0)
    return pl.pallas_call(
        functools.partial(_ffn_kernel, final_norm=final_norm),
        grid=(bsz, nt),
        in_specs=[
            pl.BlockSpec((FFN_TM, D_MODEL), rows),
            pl.BlockSpec((FFN_TM, kp), rows),
            _const_spec((kp, D_MODEL)),
            _const_spec((1, D_MODEL)),
            _const_spec((D_MODEL, 2 * FFN_DIM)),
            _const_spec((FFN_CONV, 2 * FFN_DIM)),
            _const_spec((FFN_DIM, D_MODEL)),
            _const_spec((1, D_MODEL)),
        ],
        out_specs=pl.BlockSpec((FFN_TM, D_MODEL), rows),
        out_shape=jax.ShapeDtypeStruct((bsz * seq, D_MODEL), F32),
        scratch_shapes=[
            pltpu.VMEM((2, SUBLANES + FFN_TM, FFN_COL), F32),
            pltpu.VMEM((SUBLANES, 2 * FFN_DIM), F32),
        ],
        compiler_params=_params(("parallel", "arbitrary")),
        name="ffn_final" if final_norm else "ffn",
    )(x2, pre, wpre, nw, wup, cw, wdn, fnw)


QKV_TM = 512
QKV_SUB = ATT_TILE // QKV_TM


def _rotary(v, cos, sin, lane):
    swapped = jnp.where(lane < ROPE_DIM // 2,
                        pltpu.roll(v, LANES - ROPE_DIM // 2, axis=1),
                        pltpu.roll(v, ROPE_DIM // 2, axis=1))
    return v * cos + swapped * sin


def _qkv_kernel(x_ref, kvn_ref, qn_ref, wq_ref, wkv_ref, cos_ref, sin_ref,
                q0_ref, q1_ref, q2_ref, k0_ref, k1_ref, k2_ref, v0_ref, v1_ref, v2_ref, scr_ref):
    tm = QKV_TM
    x = x_ref[...]
    xn = x * lax.rsqrt(jnp.mean(x * x, axis=-1, keepdims=True) + RMS_EPS)
    h_kv = (xn * kvn_ref[...]).astype(BF16)
    h_q = (xn * qn_ref[...]).astype(BF16)
    cos = cos_ref[...]
    sin = sin_ref[...]
    lane = lax.broadcasted_iota(jnp.int32, (tm, LANES), 1)
    q_scale = ATT_HEAD_DIM ** -0.5

    def emit(res, out_ref, d):
        if d == 1:
            out_ref[...] = res.astype(BF16)
            return
        n_ct = res.shape[1] // LANES
        for ct in range(n_ct):
            scr_ref[ct] = res[:, ct * LANES:(ct + 1) * LANES]
        for r in range(d):
            for ct in range(n_ct):
                out_ref[r, :, ct * LANES:(ct + 1) * LANES] = scr_ref[ct, pl.ds(r, tm // d, stride=d), :].astype(BF16)

    q_refs = (q0_ref, q1_ref, q2_ref)
    k_refs = (k0_ref, k1_ref, k2_ref)
    v_refs = (v0_ref, v1_ref, v2_ref)
    for g, d in enumerate(ATT_DILATIONS):
        qg = jnp.dot(h_q, wq_ref[:, g * ATT_Q_GROUP_DIM:(g + 1) * ATT_Q_GROUP_DIM], preferred_element_type=F32)
        qg = jnp.concatenate(
            [_rotary(qg[:, hd * LANES:(hd + 1) * LANES], cos, sin, lane) for hd in range(ATT_HEADS)], axis=1)
        emit(qg * q_scale, q_refs[g], d)
        kc = g * ATT_KV_GROUP_DIM
        kg = jnp.dot(h_kv, wkv_ref[:, kc:kc + ATT_KV_GROUP_DIM], preferred_element_type=F32)
        kg = jnp.concatenate(
            [_rotary(kg[:, hd * LANES:(hd + 1) * LANES], cos, sin, lane) for hd in range(ATT_KV_HEADS)], axis=1)
        emit(kg, k_refs[g], d)
        vc = len(ATT_DILATIONS) * ATT_KV_GROUP_DIM + g * ATT_KV_GROUP_DIM
        vg = jnp.dot(h_kv, wkv_ref[:, vc:vc + ATT_KV_GROUP_DIM], preferred_element_type=F32)
        emit(vg, v_refs[g], d)


def _qkv(x2, kvn, qn, wq, wkv, cos_t, sin_t, bsz, seq):
    tm = QKV_TM
    nt = seq // tm
    n_att = seq // ATT_TILE

    def out_arrays(cols):
        shapes, specs = [], []
        for d in ATT_DILATIONS:
            if d == 1:
                shapes.append(jax.ShapeDtypeStruct((bsz * seq, cols), BF16))
                specs.append(pl.BlockSpec((tm, cols), lambda b, s: (b * nt + s, 0)))
            else:
                shapes.append(jax.ShapeDtypeStruct((bsz, n_att, d, QKV_SUB, tm // d, cols), BF16))
                specs.append(pl.BlockSpec((None, None, d, None, tm // d, cols),
                                          lambda b, s: (b, s // QKV_SUB, 0, s % QKV_SUB, 0, 0)))
        return shapes, specs

    q_shapes, q_specs = out_arrays(ATT_Q_GROUP_DIM)
    k_shapes, k_specs = out_arrays(ATT_KV_GROUP_DIM)
    v_shapes, v_specs = out_arrays(ATT_KV_GROUP_DIM)
    outs = pl.pallas_call(
        _qkv_kernel,
        grid=(bsz, nt),
        in_specs=[
            pl.BlockSpec((tm, D_MODEL), lambda b, s: (b * nt + s, 0)),
            _const_spec((1, D_MODEL)),
            _const_spec((1, D_MODEL)),
            _const_spec(wq.shape),
            _const_spec(wkv.shape),
            pl.BlockSpec((tm, LANES), lambda b, s: (s, 0)),
            pl.BlockSpec((tm, LANES), lambda b, s: (s, 0)),
        ],
        out_specs=q_specs + k_specs + v_specs,
        out_shape=q_shapes + k_shapes + v_shapes,
        scratch_shapes=[pltpu.VMEM((ATT_Q_GROUP_DIM // LANES, tm, LANES), F32)],
        compiler_params=_params(("parallel", "parallel")),
        name="qkv",
    )(x2, kvn, qn, wq, wkv, cos_t, sin_t)
    return [o.reshape(bsz, seq, o.shape[-1]) for o in outs]


def _attn_kernel(q0_ref, q1_ref, q2_ref,
                 k0c_ref, k1c_ref, k2c_ref, v0c_ref, v1c_ref, v2c_ref,
                 k0p_ref, k1p_ref, k2p_ref, v0p_ref, v1p_ref, v2p_ref,
                 o_ref, acc_ref, m_ref, l_ref, bias_ref):
    blk = ATT_BLOCK
    rows_h = ATT_REP * blk
    no_prev_tile = (pl.program_id(1) == 0).astype(jnp.int32)

    qi = lax.broadcasted_iota(jnp.int32, (blk, 2 * blk), 0)
    si = lax.broadcasted_iota(jnp.int32, (blk, 2 * blk), 1)
    allowed = ((si < blk) & (si >= qi)) | ((si >= blk) & (si - blk <= qi))
    bias1 = jnp.where(allowed, 0.0, NEG_BIG).astype(F32)
    bias_ref[0] = jnp.concatenate([bias1] * ATT_REP, axis=0)
    bias_ref[1] = jnp.where(jnp.concatenate([si] * ATT_REP, axis=0) < blk, NEG_BIG, bias_ref[0])

    def unit(q_ref, kc_ref, vc_ref, kp_ref, vp_ref, u, pu, first, d, boundary):
        r0 = pl.multiple_of(u * blk, blk)
        p0 = pl.multiple_of(pu * blk, blk)
        q = jnp.concatenate([q_ref[pl.ds(r0, blk), hd * LANES:(hd + 1) * LANES] for hd in range(ATT_REP)], axis=0)
        kk = jnp.concatenate([kp_ref[pl.ds(p0, blk), :], kc_ref[pl.ds(r0, blk), :]], axis=0)
        vv = jnp.concatenate([vp_ref[pl.ds(p0, blk), :], vc_ref[pl.ds(r0, blk), :]], axis=0)
        s = lax.dot_general(q, kk, (((1,), (1,)), ((), ())), preferred_element_type=F32)
        if boundary:
            s = s + bias_ref[no_prev_tile]
        else:
            s = s + bias_ref[0]
        m_cur = jnp.max(s, axis=-1, keepdims=True)
        per_res = ATT_TILE // (blk * d)
        start = (u % per_res) * (blk * d) + u // per_res
        if d == 1:
            pos = pl.ds(r0, blk)
        else:
            pos = pl.ds(start, blk, stride=d)
        if first:
            m_new = jnp.broadcast_to(m_cur, (rows_h, LANES))
            p = jnp.exp(s - m_cur)
            l_new = jnp.broadcast_to(jnp.sum(p, axis=-1, keepdims=True), (rows_h, LANES))
            acc = jnp.dot(p.astype(BF16), vv, preferred_element_type=F32)
        else:
            m_old = jnp.concatenate([m_ref[hd, pos, :] for hd in range(ATT_REP)], axis=0)
            l_old = jnp.concatenate([l_ref[hd, pos, :] for hd in range(ATT_REP)], axis=0)
            a_old = jnp.concatenate([acc_ref[hd, pos, :] for hd in range(ATT_REP)], axis=0)
            m_new = jnp.maximum(m_old, m_cur)
            alpha = jnp.exp(m_old - m_new)
            p = jnp.exp(jnp.concatenate([s[:, :blk] - m_new, s[:, blk:] - m_new], axis=1))
            l_new = alpha * l_old + jnp.sum(p, axis=-1, keepdims=True)
            acc = alpha * a_old + jnp.dot(p.astype(BF16), vv, preferred_element_type=F32)
        for hd in range(ATT_REP):
            hs = slice(hd * blk, (hd + 1) * blk)
            m_ref[hd, pos, :] = m_new[hs]
            l_ref[hd, pos, :] = l_new[hs]
            acc_ref[hd, pos, :] = acc[hs]

    n_units = ATT_TILE // blk
    groups = ((q0_ref, k0c_ref, v0c_ref, k0p_ref, v0p_ref), (q1_ref, k1c_ref, v1c_ref, k1p_ref, v1p_ref),
              (q2_ref, k2c_ref, v2c_ref, k2p_ref, v2p_ref))
    for g, d in enumerate(ATT_DILATIONS):
        q_ref, kc_ref, vc_ref, kp_ref, vp_ref = groups[g]
        per_res = n_units // d
        first = g == 0

        def boundary_unit(r, _, q_ref=q_ref, kc_ref=kc_ref, vc_ref=vc_ref, kp_ref=kp_ref, vp_ref=vp_ref,
                          per_res=per_res, first=first, d=d):
            u = r * per_res
            unit(q_ref, kc_ref, vc_ref, kp_ref, vp_ref, u, u + per_res - 1, first, d, True)
            return 0

        def inner_unit(t, _, q_ref=q_ref, kc_ref=kc_ref, vc_ref=vc_ref, per_res=per_res, first=first, d=d):
            u = (t // (per_res - 1)) * per_res + t % (per_res - 1) + 1
            unit(q_ref, kc_ref, vc_ref, kc_ref, vc_ref, u, u - 1, first, d, False)
            return 0

        lax.fori_loop(0, d, boundary_unit, 0)
        if per_res > 1:
            lax.fori_loop(0, d * (per_res - 1), inner_unit, 0)

    for hd in range(ATT_REP):
        for r in range(0, ATT_TILE, 256):
            o_ref[r:r + 256, hd * LANES:(hd + 1) * LANES] = (
                acc_ref[hd, r:r + 256, :] / l_ref[hd, r:r + 256, :]).astype(BF16)


def _attn(qs, ks, vs, bsz, seq):
    n_att = seq // ATT_TILE
    cur = lambda b, i, j: (b, i, j)
    prev = lambda b, i, j: (b, jnp.maximum(i - 1, 0), j)
    q_spec = pl.BlockSpec((None, ATT_TILE, ATT_REP * ATT_HEAD_DIM), cur)
    kv_cur = pl.BlockSpec((None, ATT_TILE, ATT_HEAD_DIM), cur)
    kv_prev = pl.BlockSpec((None, ATT_TILE, ATT_HEAD_DIM), prev)
    width = ATT_REP * ATT_HEAD_DIM
    return pl.pallas_call(
        _attn_kernel,
        grid=(bsz, n_att, ATT_KV_HEADS),
        in_specs=[q_spec] * 3 + [kv_cur] * 6 + [kv_prev] * 6,
        out_specs=pl.BlockSpec((None, ATT_TILE, width), cur),
        out_shape=jax.ShapeDtypeStruct((bsz, seq, ATT_HEADS * ATT_HEAD_DIM), BF16),
        scratch_shapes=[
            pltpu.VMEM((ATT_REP, ATT_TILE, ATT_HEAD_DIM), F32),
            pltpu.VMEM((ATT_REP, ATT_TILE, ATT_HEAD_DIM), F32),
            pltpu.VMEM((ATT_REP, ATT_TILE, ATT_HEAD_DIM), F32),
            pltpu.VMEM((2, ATT_REP * ATT_BLOCK, 2 * ATT_BLOCK), F32),
        ],
        compiler_params=_params(("parallel", "parallel", "parallel")),
        name="attn",
    )(*qs, *ks, *vs, *ks, *vs)


def _rope_tables(seq):
    half = ROPE_DIM // 2
    inv_freq = jnp.power(jnp.float32(ROPE_THETA), -jnp.arange(0, ROPE_DIM, 2, dtype=F32) / ROPE_DIM)
    ang = jnp.arange(seq, dtype=jnp.int32).astype(F32)[:, None] * inv_freq[None, :]
    cos, sin = jnp.cos(ang), jnp.sin(ang)
    ones = jnp.ones((seq, LANES - ROPE_DIM), F32)
    cos_t = jnp.concatenate([cos, cos, ones], axis=1)
    sin_t = jnp.concatenate([-sin, sin, jnp.zeros_like(ones)], axis=1)
    return cos_t, sin_t


def kernel(x, a_norm, ssm_w_in, ssm_conv_w, ssm_conv_b, ssm_dt_bias, ssm_a_log, ssm_d, ssm_norm, ssm_w_out,
           kv_norm, w_kv, b_norm, att_w_q, att_w_o, ffn_norm, ffn_w_up, ffn_conv_w, ffn_w_down, final_norm):
    bsz, seq, dm = x.shape
    assert dm == D_MODEL and seq % ATT_TILE == 0
    assert a_norm.shape[0] == 1 and b_norm.shape[0] == 1 and ffn_norm.shape[0] == 2
    t = bsz * seq
    x2 = x.reshape(t, dm)
    row = lambda v: v.reshape(1, -1).astype(F32)

    w_in = ssm_w_in[0]
    wz = w_in[:, :SSM_D_INNER].astype(BF16)
    wx = w_in[:, SSM_D_INNER:SSM_D_INNER + SSM_CONV_DIM].astype(BF16)
    wdt = jnp.pad(w_in[:, SSM_D_INNER + SSM_CONV_DIM:], ((0, 0), (0, LANES - SSM_N_HEADS)))
    pad_heads = lambda v: jnp.pad(row(v), ((0, 0), (0, LANES - SSM_N_HEADS)))
    z, xbc, dt = _in_proj(x2, row(a_norm[0]), wz, wx, wdt)
    y = _ssd(xbc, z, dt, ssm_conv_w[0], row(ssm_conv_b[0]), pad_heads(ssm_dt_bias[0]), pad_heads(ssm_a_log[0]),
             row(jnp.repeat(ssm_d[0], SSM_HEAD_DIM)), row(ssm_norm[0]), bsz, seq)
    x2 = _ffn(x2, y, ssm_w_out[0].astype(BF16), row(ffn_norm[0]), ffn_w_up[0].astype(BF16), ffn_conv_w[0],
              ffn_w_down[0].astype(BF16), row(final_norm), bsz, seq, False)

    cos_t, sin_t = _rope_tables(seq)
    outs = _qkv(x2, row(kv_norm), row(b_norm[0]), att_w_q[0].astype(BF16), w_kv.astype(BF16), cos_t, sin_t, bsz, seq)
    o = _attn(outs[0:3], outs[3:6], outs[6:9], bsz, seq)
    x2 = _ffn(x2, o.reshape(t, dm), att_w_o[0].astype(BF16), row(ffn_norm[1]), ffn_w_up[1].astype(BF16),
              ffn_conv_w[1], ffn_w_down[1].astype(BF16), row(final_norm), bsz, seq, True)
    return x2.reshape(bsz, seq, dm)
```

```python
import functools

import jax
import jax.numpy as jnp
from jax import lax
from jax.experimental import pallas as pl
from jax.experimental.pallas import tpu as pltpu

F32 = jnp.float32
BF16 = jnp.bfloat16

D_MODEL = 1024
RMS_EPS = 1e-6
GATED_NORM_EPS = 1e-5

SSM_D_INNER = 2048
SSM_HEAD_DIM = 64
SSM_N_HEADS = 32
SSM_N_GROUPS = 8
SSM_HEADS_PER_GROUP = SSM_N_HEADS // SSM_N_GROUPS
SSM_GROUP_CH = SSM_D_INNER // SSM_N_GROUPS
SSM_D_STATE = 128
SSM_CONV = 4
SSM_CHUNK = 128
SSM_BC_DIM = SSM_N_GROUPS * SSM_D_STATE
SSM_CONV_DIM = SSM_D_INNER + 2 * SSM_BC_DIM

ATT_DILATIONS = (1, 4, 16)
ATT_BLOCK = 128
ATT_HEAD_DIM = 128
ATT_HEADS = 8
ATT_KV_HEADS = 2
ATT_REP = ATT_HEADS // ATT_KV_HEADS
ATT_TILE = ATT_BLOCK * max(ATT_DILATIONS)
ATT_UNITS = ATT_TILE // ATT_BLOCK
ATT_Q_GROUP_DIM = ATT_HEADS * ATT_HEAD_DIM
ATT_KV_GROUP_DIM = ATT_KV_HEADS * ATT_HEAD_DIM
ROPE_DIM = 32
ROPE_THETA = 500000.0

FFN_DIM = 2816
FFN_CONV = 3
FFN_COL = 256

LANES = 128
SUBLANES = 8
NEG_BIG = -1e30
VMEM_LIMIT = 56 * 1024 * 1024

ROW_TILE = 512
ROW_GROUPS = ROW_TILE // SUBLANES

HIGHEST = lax.Precision.HIGHEST


def _const_spec(shape):
    nd = len(shape)
    return pl.BlockSpec(shape, lambda *_: (0,) * nd, pipeline_mode=pl.Buffered(1))


def _params(sem):
    return pltpu.CompilerParams(dimension_semantics=sem, vmem_limit_bytes=VMEM_LIMIT)


def _silu(v):
    return v / (1.0 + jnp.exp(-v))


def _rms_normed(x, w):
    return x * lax.rsqrt(jnp.mean(x * x, axis=-1, keepdims=True) + RMS_EPS) * w


def _permute_rows(v, scr_ref):
    n_ct = v.shape[1] // LANES
    for ct in range(n_ct):
        for s in range(SUBLANES):
            scr_ref[ct, pl.ds(s, ROW_GROUPS, stride=SUBLANES), :] = (
                v[s * ROW_GROUPS:(s + 1) * ROW_GROUPS, ct * LANES:(ct + 1) * LANES])
    return jnp.concatenate([scr_ref[ct] for ct in range(n_ct)], axis=1)


def _unpermute_rows(v, scr_ref, emit):
    n_ct = v.shape[1] // LANES
    for ct in range(n_ct):
        scr_ref[ct] = v[:, ct * LANES:(ct + 1) * LANES]
    for ct in range(n_ct):
        for s in range(SUBLANES):
            emit(s, ct, scr_ref[ct, pl.ds(s, ROW_GROUPS, stride=SUBLANES), :])


def _causal_conv_permuted(u, prev_tail, taps, n_taps):
    halo = n_taps - 1
    tail = u[ROW_TILE - halo * SUBLANES:, :]
    sub = lax.broadcasted_iota(jnp.int32, (SUBLANES, u.shape[1]), 0)
    fix = []
    for i in range(halo):
        rows = slice(i * SUBLANES, (i + 1) * SUBLANES)
        fix.append(jnp.where(sub == 0, pltpu.roll(prev_tail[rows], 1, axis=0), pltpu.roll(tail[rows], 1, axis=0)))
    fix = jnp.concatenate(fix, axis=0)
    out = taps[halo] * u
    for k in range(1, halo + 1):
        shifted = jnp.concatenate([fix[(halo - k) * SUBLANES:], u[:ROW_TILE - k * SUBLANES]], axis=0)
        out = out + taps[halo - k] * shifted
    return out, tail


IN_NC = 512
SSM_HALO_ROWS = (SSM_CONV - 1) * SUBLANES


def _in_proj_kernel(x_ref, nw_ref, wz_ref, wx_ref, wdt_ref, cw_ref, cb_ref, z_ref, xbc_ref, dt_ref,
                    perm_ref, out_ref, tail_ref):
    @pl.when(pl.program_id(1) == 0)
    def _():
        tail_ref[...] = jnp.zeros_like(tail_ref)

    h = _rms_normed(x_ref[...], nw_ref[...])
    hb = h.astype(BF16)
    for c in range(0, SSM_D_INNER, IN_NC):
        z_ref[:, c:c + IN_NC] = jnp.dot(hb, wz_ref[:, c:c + IN_NC], preferred_element_type=F32).astype(BF16)
    dt_ref[...] = jnp.dot(h, wdt_ref[...], preferred_element_type=F32, precision=HIGHEST)

    hp = _permute_rows(h, perm_ref).astype(BF16)
    for i, c in enumerate(range(0, SSM_CONV_DIM, IN_NC)):
        cols = slice(c, c + IN_NC)
        u = jnp.dot(hp, wx_ref[:, cols], preferred_element_type=F32)
        taps = [cw_ref[k:k + 1, cols] for k in range(SSM_CONV)]
        conv, tail = _causal_conv_permuted(u, tail_ref[:, cols], taps, SSM_CONV)
        tail_ref[:, cols] = tail
        act = _silu(conv + cb_ref[:, cols])

        def emit(s, ct, block, c=c):
            xbc_ref[s * ROW_GROUPS:(s + 1) * ROW_GROUPS, c + ct * LANES:c + (ct + 1) * LANES] = block.astype(BF16)

        _unpermute_rows(act, out_ref.at[i % 2], emit)


def _in_proj(x2, nw, wz, wx, wdt, cw, cb, bsz, seq):
    nt = seq // ROW_TILE
    rows = lambda b, s: (b * nt + s, 0)
    t = bsz * seq
    return pl.pallas_call(
        _in_proj_kernel,
        grid=(bsz, nt),
        in_specs=[
            pl.BlockSpec((ROW_TILE, D_MODEL), rows),
            _const_spec((1, D_MODEL)),
            _const_spec((D_MODEL, SSM_D_INNER)),
            _const_spec((D_MODEL, SSM_CONV_DIM)),
            _const_spec((D_MODEL, LANES)),
            _const_spec((SSM_CONV, SSM_CONV_DIM)),
            _const_spec((1, SSM_CONV_DIM)),
        ],
        out_specs=[
            pl.BlockSpec((ROW_TILE, SSM_D_INNER), rows),
            pl.BlockSpec((ROW_TILE, SSM_CONV_DIM), rows),
            pl.BlockSpec((ROW_TILE, LANES), rows),
        ],
        out_shape=[
            jax.ShapeDtypeStruct((t, SSM_D_INNER), BF16),
            jax.ShapeDtypeStruct((t, SSM_CONV_DIM), BF16),
            jax.ShapeDtypeStruct((t, LANES), F32),
        ],
        scratch_shapes=[
            pltpu.VMEM((D_MODEL // LANES, ROW_TILE, LANES), F32),
            pltpu.VMEM((2, IN_NC // LANES, ROW_TILE, LANES), F32),
            pltpu.VMEM((SSM_HALO_ROWS, SSM_CONV_DIM), F32),
        ],
        compiler_params=_params(("parallel", "arbitrary")),
        name="in_proj",
    )(x2, nw, wz, wx, wdt, cw, cb)


def _split_hi_lo(v):
    hi = v.astype(BF16)
    lo = (v - hi.astype(F32)).astype(BF16)
    return jnp.concatenate([hi, lo], axis=1)


def _ssd_kernel(xbc_ref, z_ref, dt_ref, dtb_ref, alog_ref, dsk_ref, nw_ref, exp_ref, y_ref, state_ref):
    L = SSM_CHUNK
    P = SSM_HEAD_DIM

    @pl.when(pl.program_id(1) == 0)
    def _():
        state_ref[...] = jnp.zeros_like(state_ref)

    dt_raw = dt_ref[...] + dtb_ref[...]
    dt = jnp.maximum(dt_raw, 0.0) + jnp.log1p(jnp.exp(-jnp.abs(dt_raw)))
    adt = dt * (-jnp.exp(alog_ref[...]))
    row = lax.broadcasted_iota(jnp.int32, (L, L), 0)
    col = lax.broadcasted_iota(jnp.int32, (L, L), 1)
    causal = row >= col
    tril = jnp.where(causal, 1.0, 0.0).astype(BF16)
    a_hi = adt.astype(BF16)
    a_r1 = adt - a_hi.astype(F32)
    a_mid = a_r1.astype(BF16)
    a_lo = (a_r1 - a_mid.astype(F32)).astype(BF16)
    a_cs = (jnp.dot(tril, a_hi, preferred_element_type=F32) + jnp.dot(tril, a_mid, preferred_element_type=F32)
            + jnp.dot(tril, a_lo, preferred_element_type=F32))
    a_cs_t = a_cs.T
    a_last = a_cs[L - 1:L, :]
    w_state = dt * jnp.exp(a_last - a_cs)
    e_last = jnp.broadcast_to(jnp.exp(a_last), (2 * SUBLANES, LANES))
    expanded = jnp.dot(_split_hi_lo(jnp.concatenate([dt, w_state, e_last], axis=0)), exp_ref[...],
                       preferred_element_type=F32)
    lane_head = lax.broadcasted_iota(jnp.int32, (L, SSM_GROUP_CH), 1) // P

    for g in range(SSM_N_GROUPS):
        xcols = slice(g * SSM_GROUP_CH, (g + 1) * SSM_GROUP_CH)
        x_g = xbc_ref[:, xcols].astype(F32)
        b_bf = xbc_ref[:, SSM_D_INNER + g * SSM_D_STATE:SSM_D_INNER + (g + 1) * SSM_D_STATE]
        c_bf = xbc_ref[:, SSM_D_INNER + SSM_BC_DIM + g * SSM_D_STATE:
                       SSM_D_INNER + SSM_BC_DIM + (g + 1) * SSM_D_STATE]
        c_f = c_bf.astype(F32)
        cb = lax.dot_general(c_bf, b_bf, (((1,), (1,)), ((), ())), preferred_element_type=F32)
        prev = state_ref[g]
        xdt = (x_g * expanded[0:L, xcols]).astype(BF16)
        xw = (x_g * expanded[L:2 * L, xcols]).astype(BF16)
        rhs = jnp.concatenate([xdt, prev.astype(BF16)], axis=0)

        lhs = []
        for hh in range(SSM_HEADS_PER_GROUP):
            h = g * SSM_HEADS_PER_GROUP + hh
            a_col = jnp.broadcast_to(a_cs[:, h:h + 1], (L, L))
            decay = jnp.exp(jnp.where(causal, a_col - a_cs_t[h:h + 1, :], NEG_BIG))
            lhs.append(jnp.concatenate([(cb * decay).astype(BF16), (c_f * jnp.exp(a_col)).astype(BF16)], axis=1))
        out = jnp.dot(jnp.concatenate(lhs, axis=0), rhs, preferred_element_type=F32)
        y_g = out[(SSM_HEADS_PER_GROUP - 1) * L:]
        for hh in range(SSM_HEADS_PER_GROUP - 2, -1, -1):
            y_g = jnp.where(lane_head == hh, out[hh * L:(hh + 1) * L], y_g)
        y_g = y_g + x_g * dsk_ref[:, xcols]

        new_state = jnp.dot(b_bf.T, xw, preferred_element_type=F32)
        state_ref[g] = prev * expanded[2 * L:2 * L + 1, xcols] + new_state

        y_g = y_g * _silu(z_ref[:, xcols].astype(F32))
        y_g = y_g * lax.rsqrt(jnp.mean(y_g * y_g, axis=-1, keepdims=True) + GATED_NORM_EPS)
        y_ref[:, xcols] = (y_g * nw_ref[:, xcols]).astype(BF16)


def _ssd(xbc, z, dt, dtb, alog, dsk, nw, expand, bsz, seq):
    L = SSM_CHUNK
    nc = seq // L
    rows = lambda b, c: (b * nc + c, 0)
    return pl.pallas_call(
        _ssd_kernel,
        grid=(bsz, nc),
        in_specs=[
            pl.BlockSpec((L, SSM_CONV_DIM), rows),
            pl.BlockSpec((L, SSM_D_INNER), rows),
            pl.BlockSpec((L, LANES), rows),
            _const_spec((1, LANES)),
            _const_spec((1, LANES)),
            _const_spec((1, SSM_D_INNER)),
            _const_spec((1, SSM_D_INNER)),
            _const_spec((2 * LANES, SSM_D_INNER)),
        ],
        out_specs=pl.BlockSpec((L, SSM_D_INNER), rows),
        out_shape=jax.ShapeDtypeStruct((bsz * seq, SSM_D_INNER), BF16),
        scratch_shapes=[pltpu.VMEM((SSM_N_GROUPS, SSM_D_STATE, SSM_GROUP_CH), F32)],
        compiler_params=_params(("parallel", "arbitrary")),
        name="ssd",
    )(xbc, z, dt, dtb, alog, dsk, nw, expand)


FFN_HALO_ROWS = (FFN_CONV - 1) * SUBLANES


def _ffn_kernel(x_ref, pre_ref, wpre_ref, nw_ref, wup_ref, cw_ref, wdn_ref, fnw_ref, o_ref,
                perm_ref, tail_ref, *, final_norm):
    @pl.when(pl.program_id(1) == 0)
    def _():
        tail_ref[...] = jnp.zeros_like(tail_ref)

    x1 = x_ref[...] + jnp.dot(pre_ref[...], wpre_ref[...], preferred_element_type=F32)
    hp = _permute_rows(_rms_normed(x1, nw_ref[...]), perm_ref).astype(BF16)
    acc = None
    for c in range(FFN_DIM // FFN_COL):
        halves = []
        for half in range(2):
            off = half * FFN_DIM + c * FFN_COL
            cols = slice(off, off + FFN_COL)
            u = jnp.dot(hp, wup_ref[:, cols], preferred_element_type=F32)
            taps = [cw_ref[k:k + 1, cols] for k in range(FFN_CONV)]
            v, tail = _causal_conv_permuted(u, tail_ref[:, cols], taps, FFN_CONV)
            tail_ref[:, cols] = tail
            halves.append(v)
        act = (_silu(halves[0]) * halves[1]).astype(BF16)
        part = jnp.dot(act, wdn_ref[c * FFN_COL:(c + 1) * FFN_COL, :], preferred_element_type=F32)
        acc = part if acc is None else acc + part

    def emit(s, ct, block):
        o_ref[s * ROW_GROUPS:(s + 1) * ROW_GROUPS, ct * LANES:(ct + 1) * LANES] = block

    _unpermute_rows(acc, perm_ref, emit)
    out = x1 + o_ref[...]
    if final_norm:
        out = _rms_normed(out, fnw_ref[...])
    o_ref[...] = out


def _ffn(x2, pre, wpre, nw, wup, cw, wdn, fnw, bsz, seq, final_norm):
    nt = seq // ROW_TILE
    kp = pre.shape[1]
    rows = lambda b, s: (b * nt + s, 0)
    return pl.pallas_call(
        functools.partial(_ffn_kernel, final_norm=final_norm),
        grid=(bsz, nt),
        in_specs=[
            pl.BlockSpec((ROW_TILE, D_MODEL), rows),
            pl.BlockSpec((ROW_TILE, kp), rows),
            _const_spec((kp, D_MODEL)),
            _const_spec((1, D_MODEL)),
            _const_spec((D_MODEL, 2 * FFN_DIM)),
            _const_spec((FFN_CONV, 2 * FFN_DIM)),
            _const_spec((FFN_DIM, D_MODEL)),
            _const_spec((1, D_MODEL)),
        ],
        out_specs=pl.BlockSpec((ROW_TILE, D_MODEL), rows),
        out_shape=jax.ShapeDtypeStruct((bsz * seq, D_MODEL), F32),
        scratch_shapes=[
            pltpu.VMEM((D_MODEL // LANES, ROW_TILE, LANES), F32),
            pltpu.VMEM((FFN_HALO_ROWS, 2 * FFN_DIM), F32),
        ],
        compiler_params=_params(("parallel", "arbitrary")),
        name="ffn_final" if final_norm else "ffn",
    )(x2, pre, wpre, nw, wup, cw, wdn, fnw)


QKV_SUB = ATT_TILE // ROW_TILE


def _rotary(v, cos, sin, lane):
    swapped = jnp.where(lane < ROPE_DIM // 2,
                        pltpu.roll(v, LANES - ROPE_DIM // 2, axis=1),
                        pltpu.roll(v, ROPE_DIM // 2, axis=1))
    return v * cos + swapped * sin


def _qkv_kernel(x_ref, kvn_ref, qn_ref, wq_ref, wkv_ref, cos_ref, sin_ref,
                q0_ref, q1_ref, q2_ref, k0_ref, k1_ref, k2_ref, v0_ref, v1_ref, v2_ref, scr_ref):
    tm = ROW_TILE
    x = x_ref[...]
    xn = x * lax.rsqrt(jnp.mean(x * x, axis=-1, keepdims=True) + RMS_EPS)
    h_kv = (xn * kvn_ref[...]).astype(BF16)
    h_q = (xn * qn_ref[...]).astype(BF16)
    cos = cos_ref[...]
    sin = sin_ref[...]
    lane = lax.broadcasted_iota(jnp.int32, (tm, LANES), 1)
    q_scale = ATT_HEAD_DIM ** -0.5

    def emit(res, out_ref, d):
        if d == 1:
            out_ref[...] = res.astype(BF16)
            return
        n_ct = res.shape[1] // LANES
        for ct in range(n_ct):
            scr_ref[ct] = res[:, ct * LANES:(ct + 1) * LANES]
        for r in range(d):
            for ct in range(n_ct):
                out_ref[r, :, ct * LANES:(ct + 1) * LANES] = scr_ref[ct, pl.ds(r, tm // d, stride=d), :].astype(BF16)

    q_refs = (q0_ref, q1_ref, q2_ref)
    k_refs = (k0_ref, k1_ref, k2_ref)
    v_refs = (v0_ref, v1_ref, v2_ref)
    for g, d in enumerate(ATT_DILATIONS):
        qg = jnp.dot(h_q, wq_ref[:, g * ATT_Q_GROUP_DIM:(g + 1) * ATT_Q_GROUP_DIM], preferred_element_type=F32)
        qg = jnp.concatenate(
            [_rotary(qg[:, hd * LANES:(hd + 1) * LANES], cos, sin, lane) for hd in range(ATT_HEADS)], axis=1)
        emit(qg * q_scale, q_refs[g], d)
        kc = g * ATT_KV_GROUP_DIM
        kg = jnp.dot(h_kv, wkv_ref[:, kc:kc + ATT_KV_GROUP_DIM], preferred_element_type=F32)
        kg = jnp.concatenate(
            [_rotary(kg[:, hd * LANES:(hd + 1) * LANES], cos, sin, lane) for hd in range(ATT_KV_HEADS)], axis=1)
        emit(kg, k_refs[g], d)
        vc = len(ATT_DILATIONS) * ATT_KV_GROUP_DIM + g * ATT_KV_GROUP_DIM
        vg = jnp.dot(h_kv, wkv_ref[:, vc:vc + ATT_KV_GROUP_DIM], preferred_element_type=F32)
        emit(vg, v_refs[g], d)


def _qkv(x2, kvn, qn, wq, wkv, cos_t, sin_t, bsz, seq):
    tm = ROW_TILE
    nt = seq // tm
    n_att = seq // ATT_TILE

    def out_arrays(cols):
        shapes, specs = [], []
        for d in ATT_DILATIONS:
            if d == 1:
                shapes.append(jax.ShapeDtypeStruct((bsz * seq, cols), BF16))
                specs.append(pl.BlockSpec((tm, cols), lambda b, s: (b * nt + s, 0)))
            else:
                shapes.append(jax.ShapeDtypeStruct((bsz, n_att, d, QKV_SUB, tm // d, cols), BF16))
                specs.append(pl.BlockSpec((None, None, d, None, tm // d, cols),
                                          lambda b, s: (b, s // QKV_SUB, 0, s % QKV_SUB, 0, 0)))
        return shapes, specs

    q_shapes, q_specs = out_arrays(ATT_Q_GROUP_DIM)
    k_shapes, k_specs = out_arrays(ATT_KV_GROUP_DIM)
    v_shapes, v_specs = out_arrays(ATT_KV_GROUP_DIM)
    outs = pl.pallas_call(
        _qkv_kernel,
        grid=(bsz, nt),
        in_specs=[
            pl.BlockSpec((tm, D_MODEL), lambda b, s: (b * nt + s, 0)),
            _const_spec((1, D_MODEL)),
            _const_spec((1, D_MODEL)),
            _const_spec(wq.shape),
            _const_spec(wkv.shape),
            pl.BlockSpec((tm, LANES), lambda b, s: (s, 0)),
            pl.BlockSpec((tm, LANES), lambda b, s: (s, 0)),
        ],
        out_specs=q_specs + k_specs + v_specs,
        out_shape=q_shapes + k_shapes + v_shapes,
        scratch_shapes=[pltpu.VMEM((ATT_Q_GROUP_DIM // LANES, tm, LANES), F32)],
        compiler_params=_params(("parallel", "parallel")),
        name="qkv",
    )(x2, kvn, qn, wq, wkv, cos_t, sin_t)
    return [o.reshape(bsz, seq, o.shape[-1]) for o in outs]


def _ext_rows(d):
    return (ATT_UNITS // d + 1) * d * ATT_BLOCK


def _attn_kernel(q0_ref, q1_ref, q2_ref,
                 k0c_ref, k1c_ref, k2c_ref, v0c_ref, v1c_ref, v2c_ref,
                 k0p_ref, k1p_ref, k2p_ref, v0p_ref, v1p_ref, v2p_ref,
                 o_ref, acc_ref, m_ref, l_ref, bias_ref,
                 ke0_ref, ke1_ref, ke2_ref, ve0_ref, ve1_ref, ve2_ref):
    blk = ATT_BLOCK
    rows_h = ATT_REP * blk
    no_prev_tile = (pl.program_id(1) == 0).astype(jnp.int32)

    qi = lax.broadcasted_iota(jnp.int32, (blk, 2 * blk), 0)
    si = lax.broadcasted_iota(jnp.int32, (blk, 2 * blk), 1)
    allowed = ((si < blk) & (si >= qi)) | ((si >= blk) & (si - blk <= qi))
    bias1 = jnp.where(allowed, 0.0, NEG_BIG).astype(F32)
    bias_ref[0] = jnp.concatenate([bias1] * ATT_REP, axis=0)
    bias_ref[1] = jnp.where(jnp.concatenate([si] * ATT_REP, axis=0) < blk, NEG_BIG, bias_ref[0])

    groups = ((q0_ref, k0c_ref, v0c_ref, k0p_ref, v0p_ref, ke0_ref, ve0_ref),
              (q1_ref, k1c_ref, v1c_ref, k1p_ref, v1p_ref, ke1_ref, ve1_ref),
              (q2_ref, k2c_ref, v2c_ref, k2p_ref, v2p_ref, ke2_ref, ve2_ref))

    for g, d in enumerate(ATT_DILATIONS):
        _, kc_ref, vc_ref, kp_ref, vp_ref, ke_ref, ve_ref = groups[g]
        per_res = ATT_UNITS // d
        ve_ref[:, ATT_HEAD_DIM:] = jnp.ones((_ext_rows(d), ATT_HEAD_DIM), BF16)
        for r in range(d):
            dst = r * (per_res + 1) * blk
            src_prev = (r * per_res + per_res - 1) * blk
            src = r * per_res * blk
            ke_ref[dst:dst + blk, :] = kp_ref[src_prev:src_prev + blk, :]
            ve_ref[dst:dst + blk, 0:ATT_HEAD_DIM] = vp_ref[src_prev:src_prev + blk, :]
            ke_ref[dst + blk:dst + (per_res + 1) * blk, :] = kc_ref[src:src + per_res * blk, :]
            ve_ref[dst + blk:dst + (per_res + 1) * blk, 0:ATT_HEAD_DIM] = vc_ref[src:src + per_res * blk, :]

    def unit(u, q_ref, ke_ref, ve_ref, d, first):
        per_res = ATT_UNITS // d
        r = u // per_res
        n = u % per_res
        r0 = pl.multiple_of(u * blk, blk)
        e0 = pl.multiple_of((r * (per_res + 1) + n) * blk, blk)
        q = jnp.concatenate([q_ref[pl.ds(r0, blk), hd * LANES:(hd + 1) * LANES] for hd in range(ATT_REP)], axis=0)
        s = lax.dot_general(q, ke_ref[pl.ds(e0, 2 * blk), :], (((1,), (1,)), ((), ())), preferred_element_type=F32)
        s = s + bias_ref[jnp.where(n == 0, no_prev_tile, 0)]
        m_cur = jnp.max(s, axis=-1, keepdims=True)
        p = jnp.exp(s - m_cur).astype(BF16)
        pv = jnp.dot(p, ve_ref[pl.ds(e0, 2 * blk), :], preferred_element_type=F32)
        acc_c = pv[:, :ATT_HEAD_DIM]
        l_c = pv[:, ATT_HEAD_DIM:]
        m_c = jnp.broadcast_to(m_cur, (rows_h, LANES))
        pos = pl.ds(r0, blk) if d == 1 else pl.ds(n * (blk * d) + r, blk, stride=d)
        if first:
            m_new, l_new, acc = m_c, l_c, acc_c
        else:
            m_old = jnp.concatenate([m_ref[hd, pos, :] for hd in range(ATT_REP)], axis=0)
            l_old = jnp.concatenate([l_ref[hd, pos, :] for hd in range(ATT_REP)], axis=0)
            a_old = jnp.concatenate([acc_ref[hd, pos, :] for hd in range(ATT_REP)], axis=0)
            m_new = jnp.maximum(m_old, m_c)
            w_old = jnp.exp(m_old - m_new)
            w_cur = jnp.exp(m_c - m_new)
            l_new = w_old * l_old + w_cur * l_c
            acc = w_old * a_old + w_cur * acc_c
        for hd in range(ATT_REP):
            hs = slice(hd * blk, (hd + 1) * blk)
            m_ref[hd, pos, :] = m_new[hs]
            l_ref[hd, pos, :] = l_new[hs]
            acc_ref[hd, pos, :] = acc[hs]

    for g, d in enumerate(ATT_DILATIONS):
        q_ref, _, _, _, _, ke_ref, ve_ref = groups[g]

        def body(u, carry, q_ref=q_ref, ke_ref=ke_ref, ve_ref=ve_ref, d=d, first=(g == 0)):
            unit(u, q_ref, ke_ref, ve_ref, d, first)
            return carry

        lax.fori_loop(0, ATT_UNITS, body, 0, unroll=2)

    for hd in range(ATT_REP):
        for r in range(0, ATT_TILE, 256):
            o_ref[r:r + 256, hd * LANES:(hd + 1) * LANES] = (
                acc_ref[hd, r:r + 256, :] / l_ref[hd, r:r + 256, :]).astype(BF16)


def _attn(qs, ks, vs, bsz, seq):
    n_att = seq // ATT_TILE
    cur = lambda b, i, j: (b, i, j)
    prev = lambda b, i, j: (b, jnp.maximum(i - 1, 0), j)
    q_spec = pl.BlockSpec((None, ATT_TILE, ATT_REP * ATT_HEAD_DIM), cur)
    kv_cur = pl.BlockSpec((None, ATT_TILE, ATT_HEAD_DIM), cur)
    kv_prev = pl.BlockSpec((None, ATT_TILE, ATT_HEAD_DIM), prev)
    width = ATT_REP * ATT_HEAD_DIM
    stat = pltpu.VMEM((ATT_REP, ATT_TILE, ATT_HEAD_DIM), F32)
    return pl.pallas_call(
        _attn_kernel,
        grid=(bsz, n_att, ATT_KV_HEADS),
        in_specs=[q_spec] * 3 + [kv_cur] * 6 + [kv_prev] * 6,
        out_specs=pl.BlockSpec((None, ATT_TILE, width), cur),
        out_shape=jax.ShapeDtypeStruct((bsz, seq, ATT_HEADS * ATT_HEAD_DIM), BF16),
        scratch_shapes=[stat, stat, stat,
                        pltpu.VMEM((2, ATT_REP * ATT_BLOCK, 2 * ATT_BLOCK), F32)]
        + [pltpu.VMEM((_ext_rows(d), ATT_HEAD_DIM), BF16) for d in ATT_DILATIONS]
        + [pltpu.VMEM((_ext_rows(d), 2 * ATT_HEAD_DIM), BF16) for d in ATT_DILATIONS],
        compiler_params=_params(("parallel", "parallel", "parallel")),
        name="attn",
    )(*qs, *ks, *vs, *ks, *vs)


def _rope_tables(seq):
    inv_freq = jnp.power(jnp.float32(ROPE_THETA), -jnp.arange(0, ROPE_DIM, 2, dtype=F32) / ROPE_DIM)
    ang = jnp.arange(seq, dtype=jnp.int32).astype(F32)[:, None] * inv_freq[None, :]
    cos, sin = jnp.cos(ang), jnp.sin(ang)
    ones = jnp.ones((seq, LANES - ROPE_DIM), F32)
    cos_t = jnp.concatenate([cos, cos, ones], axis=1)
    sin_t = jnp.concatenate([-sin, sin, jnp.zeros_like(ones)], axis=1)
    return cos_t, sin_t


def _head_expand():
    head = jnp.arange(LANES)[:, None]
    chan_head = (jnp.arange(SSM_D_INNER) // SSM_HEAD_DIM)[None, :]
    e = (head == chan_head).astype(BF16)
    return jnp.concatenate([e, e], axis=0)


def kernel(x, a_norm, ssm_w_in, ssm_conv_w, ssm_conv_b, ssm_dt_bias, ssm_a_log, ssm_d, ssm_norm, ssm_w_out,
           kv_norm, w_kv, b_norm, att_w_q, att_w_o, ffn_norm, ffn_w_up, ffn_conv_w, ffn_w_down, final_norm):
    bsz, seq, dm = x.shape
    assert dm == D_MODEL and seq % ATT_TILE == 0
    assert a_norm.shape[0] == 1 and b_norm.shape[0] == 1 and ffn_norm.shape[0] == 2
    t = bsz * seq
    x2 = x.reshape(t, dm)
    row = lambda v: v.reshape(1, -1).astype(F32)

    w_in = ssm_w_in[0]
    wz = w_in[:, :SSM_D_INNER].astype(BF16)
    wx = w_in[:, SSM_D_INNER:SSM_D_INNER + SSM_CONV_DIM].astype(BF16)
    wdt = jnp.pad(w_in[:, SSM_D_INNER + SSM_CONV_DIM:], ((0, 0), (0, LANES - SSM_N_HEADS)))
    pad_heads = lambda v: jnp.pad(row(v), ((0, 0), (0, LANES - SSM_N_HEADS)))
    z, xbc, dt = _in_proj(x2, row(a_norm[0]), wz, wx, wdt, ssm_conv_w[0], row(ssm_conv_b[0]), bsz, seq)
    y = _ssd(xbc, z, dt, pad_heads(ssm_dt_bias[0]), pad_heads(ssm_a_log[0]),
             row(jnp.repeat(ssm_d[0], SSM_HEAD_DIM)), row(ssm_norm[0]), _head_expand(), bsz, seq)
    x2 = _ffn(x2, y, ssm_w_out[0].astype(BF16), row(ffn_norm[0]), ffn_w_up[0].astype(BF16), ffn_conv_w[0],
              ffn_w_down[0].astype(BF16), row(final_norm), bsz, seq, False)

    cos_t, sin_t = _rope_tables(seq)
    outs = _qkv(x2, row(kv_norm), row(b_norm[0]), att_w_q[0].astype(BF16), w_kv.astype(BF16), cos_t, sin_t, bsz, seq)
    o = _attn(outs[0:3], outs[3:6], outs[6:9], bsz, seq)
    x2 = _ffn(x2, o.reshape(t, dm), att_w_o[0].astype(BF16), row(ffn_norm[1]), ffn_w_up[1].astype(BF16),
              ffn_conv_w[1], ffn_w_down[1].astype(BF16), row(final_norm), bsz, seq, True)
    return x2.reshape(bsz, seq, dm)
```

```python
import functools
import math

import jax
import jax.numpy as jnp
from jax import lax
from jax.experimental import pallas as pl
from jax.experimental.pallas import tpu as pltpu

F32 = jnp.float32
BF16 = jnp.bfloat16

D_MODEL = 1024
RMS_EPS = 1e-6
GATED_NORM_EPS = 1e-5

SSM_D_INNER = 2048
SSM_HEAD_DIM = 64
SSM_N_HEADS = 32
SSM_N_GROUPS = 8
SSM_HEADS_PER_GROUP = SSM_N_HEADS // SSM_N_GROUPS
SSM_GROUP_CH = SSM_D_INNER // SSM_N_GROUPS
SSM_D_STATE = 128
SSM_CONV = 4
SSM_CHUNK = 128
SSM_BC_DIM = SSM_N_GROUPS * SSM_D_STATE
SSM_CONV_DIM = SSM_D_INNER + 2 * SSM_BC_DIM

ATT_DILATIONS = (1, 4, 16)
ATT_MAX_DIL = max(ATT_DILATIONS)
ATT_BLOCK = 128
ATT_HEAD_DIM = 128
ATT_HEADS = 8
ATT_KV_HEADS = 2
ATT_REP = ATT_HEADS // ATT_KV_HEADS
ATT_TILE = ATT_BLOCK * ATT_MAX_DIL
ATT_UNITS = ATT_TILE // ATT_BLOCK
ATT_Q_GROUP_DIM = ATT_HEADS * ATT_HEAD_DIM
ATT_KV_GROUP_DIM = ATT_KV_HEADS * ATT_HEAD_DIM
ROPE_DIM = 32
ROPE_THETA = 500000.0

FFN_DIM = 2816
FFN_CONV = 3
FFN_COL = 256

LANES = 128
SUBLANES = 8
BF16_ROWS = 2 * SUBLANES
NEG_BIG = -1e30
VMEM_LIMIT = 56 * 1024 * 1024

ROW_TILE = 512
ROW_GROUPS = ROW_TILE // SUBLANES


def _const_spec(shape):
    nd = len(shape)
    return pl.BlockSpec(shape, lambda *_: (0,) * nd, pipeline_mode=pl.Buffered(1))


def _params(sem):
    return pltpu.CompilerParams(dimension_semantics=sem, vmem_limit_bytes=VMEM_LIMIT)


def _silu(v):
    return v / (1.0 + jnp.exp(-v))


def _log1p(v):
    u = 1.0 + v
    return jnp.where(u == 1.0, v, jnp.log(u) * (v / (u - 1.0)))


def _rms_normed(x, w):
    return x * lax.rsqrt(jnp.mean(x * x, axis=-1, keepdims=True) + RMS_EPS) * w


def _pipelined(jobs):
    pending = jobs[0][0]()
    for i, (_, epilogue) in enumerate(jobs):
        upcoming = jobs[i + 1][0]() if i + 1 < len(jobs) else None
        epilogue(pending)
        pending = upcoming


def _permute_rows(v, scr_ref):
    n_ct = v.shape[1] // LANES
    for ct in range(n_ct):
        for s in range(SUBLANES):
            scr_ref[ct, pl.ds(s, ROW_GROUPS, stride=SUBLANES), :] = (
                v[s * ROW_GROUPS:(s + 1) * ROW_GROUPS, ct * LANES:(ct + 1) * LANES])
    return jnp.concatenate([scr_ref[ct] for ct in range(n_ct)], axis=1)


def _unpermute_rows(v, scr_ref, emit):
    n_ct = v.shape[1] // LANES
    for ct in range(n_ct):
        scr_ref[ct] = v[:, ct * LANES:(ct + 1) * LANES]
    for ct in range(n_ct):
        for s in range(SUBLANES):
            emit(s, ct, scr_ref[ct, pl.ds(s, ROW_GROUPS, stride=SUBLANES), :])


def _causal_conv_permuted(u, prev_tail, taps, n_taps):
    halo = n_taps - 1
    tail = u[ROW_TILE - halo * SUBLANES:, :]
    sub = lax.broadcasted_iota(jnp.int32, (SUBLANES, u.shape[1]), 0)
    fix = []
    for i in range(halo):
        rows = slice(i * SUBLANES, (i + 1) * SUBLANES)
        fix.append(jnp.where(sub == 0, pltpu.roll(prev_tail[rows], 1, axis=0), pltpu.roll(tail[rows], 1, axis=0)))
    fix = jnp.concatenate(fix, axis=0)
    out = taps[halo] * u
    for k in range(1, halo + 1):
        shifted = jnp.concatenate([fix[(halo - k) * SUBLANES:], u[:ROW_TILE - k * SUBLANES]], axis=0)
        out = out + taps[halo - k] * shifted
    return out, tail


IN_NC = 512
SSM_HALO_ROWS = (SSM_CONV - 1) * SUBLANES


def _in_proj_kernel(x_ref, nw_ref, wz_ref, wx_ref, wdt_ref, cw_ref, cb_ref, z_ref, xbc_ref, dt_ref,
                    perm_ref, out_ref, tail_ref):
    @pl.when(pl.program_id(1) == 0)
    def _():
        tail_ref[...] = jnp.zeros_like(tail_ref)

    h = _rms_normed(x_ref[...], nw_ref[...])
    hb = h.astype(BF16)
    hp = _permute_rows(h, perm_ref).astype(BF16)

    def xbc_job(i):
        cols = slice(i * IN_NC, (i + 1) * IN_NC)

        def epilogue(u):
            taps = [cw_ref[k:k + 1, cols] for k in range(SSM_CONV)]
            conv, tail = _causal_conv_permuted(u, tail_ref[:, cols], taps, SSM_CONV)
            tail_ref[:, cols] = tail
            act = _silu(conv + cb_ref[:, cols])

            def emit(s, ct, block):
                c0 = i * IN_NC + ct * LANES
                xbc_ref[s * ROW_GROUPS:(s + 1) * ROW_GROUPS, c0:c0 + LANES] = block.astype(BF16)

            _unpermute_rows(act, out_ref.at[i % 2], emit)

        return (lambda: jnp.dot(hp, wx_ref[:, cols], preferred_element_type=F32)), epilogue

    def z_job(i):
        cols = slice(i * IN_NC, (i + 1) * IN_NC)

        def epilogue(u):
            z_ref[:, cols] = u.astype(BF16)

        return (lambda: jnp.dot(hb, wz_ref[:, cols], preferred_element_type=F32)), epilogue

    def dt_job():
        def matmul():
            h_lo = (h - hb.astype(F32)).astype(BF16)
            return jnp.dot(jnp.concatenate([hb, h_lo, hb], axis=1), wdt_ref[...], preferred_element_type=F32)

        def epilogue(u):
            dt_ref[...] = u

        return matmul, epilogue

    n_x = SSM_CONV_DIM // IN_NC
    n_z = SSM_D_INNER // IN_NC
    jobs = []
    for i in range(n_x):
        jobs.append(xbc_job(i))
        if i < n_z:
            jobs.append(z_job(i))
    jobs.append(dt_job())
    _pipelined(jobs)


def _in_proj(x2, nw, wz, wx, wdt3, cw, cb, bsz, seq):
    nt = seq // ROW_TILE
    rows = lambda b, s: (b * nt + s, 0)
    t = bsz * seq
    return pl.pallas_call(
        _in_proj_kernel,
        grid=(bsz, nt),
        in_specs=[
            pl.BlockSpec((ROW_TILE, D_MODEL), rows),
            _const_spec((1, D_MODEL)),
            _const_spec((D_MODEL, SSM_D_INNER)),
            _const_spec((D_MODEL, SSM_CONV_DIM)),
            _const_spec((3 * D_MODEL, LANES)),
            _const_spec((SSM_CONV, SSM_CONV_DIM)),
            _const_spec((1, SSM_CONV_DIM)),
        ],
        out_specs=[
            pl.BlockSpec((ROW_TILE, SSM_D_INNER), rows),
            pl.BlockSpec((ROW_TILE, SSM_CONV_DIM), rows),
            pl.BlockSpec((ROW_TILE, LANES), rows),
        ],
        out_shape=[
            jax.ShapeDtypeStruct((t, SSM_D_INNER), BF16),
            jax.ShapeDtypeStruct((t, SSM_CONV_DIM), BF16),
            jax.ShapeDtypeStruct((t, LANES), F32),
        ],
        scratch_shapes=[
            pltpu.VMEM((D_MODEL // LANES, ROW_TILE, LANES), F32),
            pltpu.VMEM((2, IN_NC // LANES, ROW_TILE, LANES), F32),
            pltpu.VMEM((SSM_HALO_ROWS, SSM_CONV_DIM), F32),
        ],
        compiler_params=_params(("parallel", "arbitrary")),
        name="in_proj",
    )(x2, nw, wz, wx, wdt3, cw, cb)


def _split_hi_lo(v):
    hi = v.astype(BF16)
    lo = (v - hi.astype(F32)).astype(BF16)
    return jnp.concatenate([hi, lo], axis=1)


def _ssd_kernel(xbc_ref, z_ref, dt_ref, dtb_ref, alog_ref, dsk_ref, nw_ref, exp_ref, y_ref, state_ref):
    L = SSM_CHUNK
    P = SSM_HEAD_DIM

    @pl.when(pl.program_id(1) == 0)
    def _():
        state_ref[...] = jnp.zeros_like(state_ref)

    dt_raw = dt_ref[...] + dtb_ref[...]
    dt = jnp.maximum(dt_raw, 0.0) + _log1p(jnp.exp(-jnp.abs(dt_raw)))
    adt = dt * (-jnp.exp(alog_ref[...]))
    row = lax.broadcasted_iota(jnp.int32, (L, L), 0)
    col = lax.broadcasted_iota(jnp.int32, (L, L), 1)
    causal = row >= col
    tril = jnp.where(causal, 1.0, 0.0).astype(BF16)
    a_hi = adt.astype(BF16)
    a_r1 = adt - a_hi.astype(F32)
    a_mid = a_r1.astype(BF16)
    a_lo = (a_r1 - a_mid.astype(F32)).astype(BF16)
    a_cs = (jnp.dot(tril, a_hi, preferred_element_type=F32) + jnp.dot(tril, a_mid, preferred_element_type=F32)
            + jnp.dot(tril, a_lo, preferred_element_type=F32))
    a_cs_t = a_cs.T
    a_last = a_cs[L - 1:L, :]
    w_state = dt * jnp.exp(a_last - a_cs)
    e_acs = jnp.exp(a_cs)
    e_last = jnp.broadcast_to(jnp.exp(a_last), (BF16_ROWS, LANES))
    expanded = jnp.dot(_split_hi_lo(jnp.concatenate([dt, w_state, e_acs, e_last], axis=0)), exp_ref[...],
                       preferred_element_type=F32)
    lane_head = lax.broadcasted_iota(jnp.int32, (L, SSM_GROUP_CH), 1) // P
    head_mask = [jnp.where(lane_head == hh, 1.0, 0.0).astype(BF16) for hh in range(SSM_HEADS_PER_GROUP)]

    for g in range(SSM_N_GROUPS):
        xcols = slice(g * SSM_GROUP_CH, (g + 1) * SSM_GROUP_CH)
        x_g = xbc_ref[:, xcols].astype(F32)
        b_bf = xbc_ref[:, SSM_D_INNER + g * SSM_D_STATE:SSM_D_INNER + (g + 1) * SSM_D_STATE]
        c_bf = xbc_ref[:, SSM_D_INNER + SSM_BC_DIM + g * SSM_D_STATE:
                       SSM_D_INNER + SSM_BC_DIM + (g + 1) * SSM_D_STATE]
        cb = lax.dot_general(c_bf, b_bf, (((1,), (1,)), ((), ())), preferred_element_type=F32)
        prev = state_ref[g]
        y_off = jnp.dot(c_bf, prev.astype(BF16), preferred_element_type=F32)
        xdt = (x_g * expanded[0:L, xcols]).astype(BF16)
        xw = (x_g * expanded[L:2 * L, xcols]).astype(BF16)

        lhs, rhs = [], []
        for hh in range(SSM_HEADS_PER_GROUP):
            h = g * SSM_HEADS_PER_GROUP + hh
            seg = jnp.broadcast_to(a_cs[:, h:h + 1], (L, L)) - a_cs_t[h:h + 1, :]
            lhs.append((cb * jnp.exp(jnp.where(causal, seg, NEG_BIG))).astype(BF16))
            rhs.append(xdt * head_mask[hh])
        y_g = jnp.dot(jnp.concatenate(lhs, axis=1), jnp.concatenate(rhs, axis=0), preferred_element_type=F32)
        y_g = y_g + y_off * expanded[2 * L:3 * L, xcols] + x_g * dsk_ref[:, xcols]

        new_state = jnp.dot(b_bf.T, xw, preferred_element_type=F32)
        state_ref[g] = prev * expanded[3 * L:3 * L + 1, xcols] + new_state

        y_g = y_g * _silu(z_ref[:, xcols].astype(F32))
        y_g = y_g * lax.rsqrt(jnp.mean(y_g * y_g, axis=-1, keepdims=True) + GATED_NORM_EPS)
        y_ref[:, xcols] = (y_g * nw_ref[:, xcols]).astype(BF16)


def _ssd(xbc, z, dt, dtb, alog, dsk, nw, expand, bsz, seq):
    L = SSM_CHUNK
    nc = seq // L
    rows = lambda b, c: (b * nc + c, 0)
    return pl.pallas_call(
        _ssd_kernel,
        grid=(bsz, nc),
        in_specs=[
            pl.BlockSpec((L, SSM_CONV_DIM), rows),
            pl.BlockSpec((L, SSM_D_INNER), rows),
            pl.BlockSpec((L, LANES), rows),
            _const_spec((1, LANES)),
            _const_spec((1, LANES)),
            _const_spec((1, SSM_D_INNER)),
            _const_spec((1, SSM_D_INNER)),
            _const_spec((2 * LANES, SSM_D_INNER)),
        ],
        out_specs=pl.BlockSpec((L, SSM_D_INNER), rows),
        out_shape=jax.ShapeDtypeStruct((bsz * seq, SSM_D_INNER), BF16),
        scratch_shapes=[pltpu.VMEM((SSM_N_GROUPS, SSM_D_STATE, SSM_GROUP_CH), F32)],
        compiler_params=_params(("parallel", "arbitrary")),
        name="ssd",
    )(xbc, z, dt, dtb, alog, dsk, nw, expand)


FFN_HALO_ROWS = (FFN_CONV - 1) * SUBLANES


def _ffn_kernel(x_ref, pre_ref, wpre_ref, nw_ref, wup_ref, cw_ref, wdn_ref, fnw_ref, o_ref,
                perm_ref, tail_ref, *, final_norm):
    @pl.when(pl.program_id(1) == 0)
    def _():
        tail_ref[...] = jnp.zeros_like(tail_ref)

    x1 = x_ref[...] + jnp.dot(pre_ref[...], wpre_ref[...], preferred_element_type=F32)
    hp = _permute_rows(_rms_normed(x1, nw_ref[...]), perm_ref).astype(BF16)
    acc = [None]

    def chunk_job(c):
        def matmul():
            return [jnp.dot(hp, wup_ref[:, half * FFN_DIM + c * FFN_COL:half * FFN_DIM + (c + 1) * FFN_COL],
                            preferred_element_type=F32) for half in range(2)]

        def epilogue(us):
            halves = []
            for half, u in enumerate(us):
                cols = slice(half * FFN_DIM + c * FFN_COL, half * FFN_DIM + (c + 1) * FFN_COL)
                taps = [cw_ref[k:k + 1, cols] for k in range(FFN_CONV)]
                v, tail = _causal_conv_permuted(u, tail_ref[:, cols], taps, FFN_CONV)
                tail_ref[:, cols] = tail
                halves.append(v)
            act = (_silu(halves[0]) * halves[1]).astype(BF16)
            part = jnp.dot(act, wdn_ref[c * FFN_COL:(c + 1) * FFN_COL, :], preferred_element_type=F32)
            acc[0] = part if acc[0] is None else acc[0] + part

        return matmul, epilogue

    _pipelined([chunk_job(c) for c in range(FFN_DIM // FFN_COL)])

    def emit(s, ct, block):
        o_ref[s * ROW_GROUPS:(s + 1) * ROW_GROUPS, ct * LANES:(ct + 1) * LANES] = block

    _unpermute_rows(acc[0], perm_ref, emit)
    out = x1 + o_ref[...]
    if final_norm:
        out = _rms_normed(out, fnw_ref[...])
    o_ref[...] = out


def _ffn(x2, pre, wpre, nw, wup, cw, wdn, fnw, bsz, seq, final_norm):
    nt = seq // ROW_TILE
    kp = pre.shape[1]
    rows = lambda b, s: (b * nt + s, 0)
    return pl.pallas_call(
        functools.partial(_ffn_kernel, final_norm=final_norm),
        grid=(bsz, nt),
        in_specs=[
            pl.BlockSpec((ROW_TILE, D_MODEL), rows),
            pl.BlockSpec((ROW_TILE, kp), rows),
            _const_spec((kp, D_MODEL)),
            _const_spec((1, D_MODEL)),
            _const_spec((D_MODEL, 2 * FFN_DIM)),
            _const_spec((FFN_CONV, 2 * FFN_DIM)),
            _const_spec((FFN_DIM, D_MODEL)),
            _const_spec((1, D_MODEL)),
        ],
        out_specs=pl.BlockSpec((ROW_TILE, D_MODEL), rows),
        out_shape=jax.ShapeDtypeStruct((bsz * seq, D_MODEL), F32),
        scratch_shapes=[
            pltpu.VMEM((D_MODEL // LANES, ROW_TILE, LANES), F32),
            pltpu.VMEM((FFN_HALO_ROWS, 2 * FFN_DIM), F32),
        ],
        compiler_params=_params(("parallel", "arbitrary")),
        name="ffn_final" if final_norm else "ffn",
    )(x2, pre, wpre, nw, wup, cw, wdn, fnw)


QKV_SUB = ATT_TILE // ROW_TILE
QKV_M = ROW_TILE // ATT_MAX_DIL


def _rotary(v, cos, sin):
    return v * cos + pltpu.roll(v, LANES // 2, axis=1) * sin


def _qkv_kernel(x_ref, kvn_ref, qn_ref, wq_ref, wkv_ref, cos_ref, sin_ref,
                q0_ref, q1_ref, q2_ref, k0_ref, k1_ref, k2_ref, v0_ref, v1_ref, v2_ref, scr_ref):
    tm = ROW_TILE
    for ct in range(D_MODEL // LANES):
        scr_ref[ct] = x_ref[:, ct * LANES:(ct + 1) * LANES]
    x = jnp.concatenate(
        [jnp.concatenate([scr_ref[ct, pl.ds(c, QKV_M, stride=ATT_MAX_DIL), :] for c in range(ATT_MAX_DIL)], axis=0)
         for ct in range(D_MODEL // LANES)], axis=1)
    xn = x * lax.rsqrt(jnp.mean(x * x, axis=-1, keepdims=True) + RMS_EPS)
    h_kv = (xn * kvn_ref[...]).astype(BF16)
    h_q = (xn * qn_ref[...]).astype(BF16)
    cos = cos_ref[...]
    sin = sin_ref[...]
    q_scale = ATT_HEAD_DIM ** -0.5 * math.log2(math.e)

    def emit(res, out_ref, d):
        by_c = [res[c * QKV_M:(c + 1) * QKV_M] for c in range(ATT_MAX_DIL)]
        if d == 16:
            for c in range(ATT_MAX_DIL):
                out_ref[c] = by_c[c].astype(BF16)
        elif d == 4:
            for c in range(ATT_MAX_DIL):
                out_ref[c % 4, c // 4] = by_c[c].astype(BF16)
        else:
            for nl in range(tm // ATT_BLOCK):
                ks = slice(nl * SUBLANES, (nl + 1) * SUBLANES)
                for c in range(0, ATT_MAX_DIL, 2):
                    r0 = nl * ATT_BLOCK + c * SUBLANES
                    out_ref[r0:r0 + BF16_ROWS, :] = jnp.concatenate(
                        [by_c[c][ks], by_c[c + 1][ks]], axis=0).astype(BF16)

    def rot(v, n_heads):
        return jnp.concatenate(
            [_rotary(v[:, hd * LANES:(hd + 1) * LANES], cos, sin) for hd in range(n_heads)], axis=1)

    q_refs = (q0_ref, q1_ref, q2_ref)
    k_refs = (k0_ref, k1_ref, k2_ref)
    v_refs = (v0_ref, v1_ref, v2_ref)
    jobs = []
    for g, d in enumerate(ATT_DILATIONS):
        qc = slice(g * ATT_Q_GROUP_DIM, (g + 1) * ATT_Q_GROUP_DIM)
        kc = slice(g * ATT_KV_GROUP_DIM, (g + 1) * ATT_KV_GROUP_DIM)
        v0 = len(ATT_DILATIONS) * ATT_KV_GROUP_DIM + g * ATT_KV_GROUP_DIM
        vc = slice(v0, v0 + ATT_KV_GROUP_DIM)
        jobs.append((lambda qc=qc: jnp.dot(h_q, wq_ref[:, qc], preferred_element_type=F32),
                     lambda u, g=g, d=d: emit(rot(u, ATT_HEADS) * q_scale, q_refs[g], d)))
        jobs.append((lambda kc=kc: jnp.dot(h_kv, wkv_ref[:, kc], preferred_element_type=F32),
                     lambda u, g=g, d=d: emit(rot(u, ATT_KV_HEADS), k_refs[g], d)))
        jobs.append((lambda vc=vc: jnp.dot(h_kv, wkv_ref[:, vc], preferred_element_type=F32),
                     lambda u, g=g, d=d: emit(u, v_refs[g], d)))
    _pipelined(jobs)


def _qkv(x2, kvn, qn, wq, wkv, cos_t, sin_t, bsz, seq):
    tm = ROW_TILE
    nt = seq // tm
    n_att = seq // ATT_TILE

    def out_arrays(cols):
        shapes, specs = [], []
        for d in ATT_DILATIONS:
            if d == 1:
                shapes.append(jax.ShapeDtypeStruct((bsz * seq, cols), BF16))
                specs.append(pl.BlockSpec((tm, cols), lambda b, s: (b * nt + s, 0)))
            elif d == 4:
                shapes.append(jax.ShapeDtypeStruct((bsz, n_att, 4, QKV_SUB, 4, QKV_M, cols), BF16))
                specs.append(pl.BlockSpec((None, None, 4, None, 4, QKV_M, cols),
                                          lambda b, s: (b, s // QKV_SUB, 0, s % QKV_SUB, 0, 0, 0)))
            else:
                shapes.append(jax.ShapeDtypeStruct((bsz, n_att, d, QKV_SUB, QKV_M, cols), BF16))
                specs.append(pl.BlockSpec((None, None, d, None, QKV_M, cols),
                                          lambda b, s: (b, s // QKV_SUB, 0, s % QKV_SUB, 0, 0)))
        return shapes, specs

    q_shapes, q_specs = out_arrays(ATT_Q_GROUP_DIM)
    k_shapes, k_specs = out_arrays(ATT_KV_GROUP_DIM)
    v_shapes, v_specs = out_arrays(ATT_KV_GROUP_DIM)
    outs = pl.pallas_call(
        _qkv_kernel,
        grid=(bsz, nt),
        in_specs=[
            pl.BlockSpec((tm, D_MODEL), lambda b, s: (b * nt + s, 0)),
            _const_spec((1, D_MODEL)),
            _const_spec((1, D_MODEL)),
            _const_spec(wq.shape),
            _const_spec(wkv.shape),
            pl.BlockSpec((tm, LANES), lambda b, s: (s, 0)),
            pl.BlockSpec((tm, LANES), lambda b, s: (s, 0)),
        ],
        out_specs=q_specs + k_specs + v_specs,
        out_shape=q_shapes + k_shapes + v_shapes,
        scratch_shapes=[pltpu.VMEM((ATT_Q_GROUP_DIM // LANES, tm, LANES), F32)],
        compiler_params=_params(("parallel", "parallel")),
        name="qkv",
    )(x2, kvn, qn, wq, wkv, cos_t, sin_t)
    return [o.reshape(bsz, seq, o.shape[-1]) for o in outs]


ATT_PITCH = ATT_BLOCK + SUBLANES
ATT_UNROLL = ATT_UNITS


def _ext_rows(d):
    return (ATT_UNITS // d + 1) * d * ATT_BLOCK


def _unit_order(g, rho):
    d = ATT_DILATIONS[g]
    if d == 1:
        return (rho % SUBLANES) * ATT_MAX_DIL + rho // SUBLANES
    if d == 4:
        return (rho % QKV_M) * 4 + rho // QKV_M
    return rho


def _state_segments(g, u):
    d = ATT_DILATIONS[g]
    if d == 1:
        return [(pl.multiple_of(c * ATT_PITCH + u * SUBLANES, SUBLANES), c * SUBLANES, SUBLANES)
                for c in range(ATT_MAX_DIL)]
    if d == 4:
        r4, n4 = u // 4, u % 4
        return [(pl.multiple_of((4 * a + r4) * ATT_PITCH + n4 * QKV_M, SUBLANES), a * QKV_M, QKV_M)
                for a in range(4)]
    return [(pl.multiple_of(u * ATT_PITCH, SUBLANES), 0, ATT_BLOCK)]


def _attn_kernel(q0_ref, q1_ref, q2_ref,
                 k0c_ref, k1c_ref, k2c_ref, v0c_ref, v1c_ref, v2c_ref,
                 k0p_ref, k1p_ref, k2p_ref, v0p_ref, v1p_ref, v2p_ref,
                 o_ref, acc_ref, m_ref, l_ref, mask_ref, eye_ref,
                 ke0_ref, ke1_ref, ke2_ref, ve0_ref, ve1_ref, ve2_ref):
    blk = ATT_BLOCK
    no_prev_tile = (pl.program_id(1) == 0).astype(jnp.int32)

    key_row = lax.broadcasted_iota(jnp.int32, (2 * blk, blk), 0)
    qry_row = lax.broadcasted_iota(jnp.int32, (2 * blk, blk), 1)
    in_prev = key_row < blk
    for g in range(len(ATT_DILATIONS)):
        i_s = _unit_order(g, key_row % blk)
        i_q = _unit_order(g, qry_row)
        cur_ok = (~in_prev) & (i_s <= i_q)
        mask_ref[2 * g] = jnp.where((in_prev & (i_s >= i_q)) | cur_ok, 0.0, NEG_BIG).astype(BF16)
        mask_ref[2 * g + 1] = jnp.where(cur_ok, 0.0, NEG_BIG).astype(BF16)
    e_row = lax.broadcasted_iota(jnp.int32, (ATT_REP * blk, blk), 0)
    e_col = lax.broadcasted_iota(jnp.int32, (ATT_REP * blk, blk), 1)
    eye_ref[...] = jnp.where(e_row % blk == e_col, 1.0, 0.0).astype(BF16)

    groups = ((q0_ref, k0c_ref, v0c_ref, k0p_ref, v0p_ref, ke0_ref, ve0_ref),
              (q1_ref, k1c_ref, v1c_ref, k1p_ref, v1p_ref, ke1_ref, ve1_ref),
              (q2_ref, k2c_ref, v2c_ref, k2p_ref, v2p_ref, ke2_ref, ve2_ref))

    for g, d in enumerate(ATT_DILATIONS):
        _, kc_ref, vc_ref, kp_ref, vp_ref, ke_ref, ve_ref = groups[g]
        per_res = ATT_UNITS // d
        ve_ref[:, ATT_HEAD_DIM:] = jnp.ones((_ext_rows(d), ATT_HEAD_DIM), BF16)
        for r in range(d):
            dst = r * (per_res + 1) * blk
            src_prev = (r * per_res + per_res - 1) * blk
            src = r * per_res * blk
            ke_ref[dst:dst + blk, :] = kp_ref[src_prev:src_prev + blk, :]
            ve_ref[dst:dst + blk, 0:ATT_HEAD_DIM] = vp_ref[src_prev:src_prev + blk, :]
            ke_ref[dst + blk:dst + (per_res + 1) * blk, :] = kc_ref[src:src + per_res * blk, :]
            ve_ref[dst + blk:dst + (per_res + 1) * blk, 0:ATT_HEAD_DIM] = vc_ref[src:src + per_res * blk, :]

    def unit(u, g, q_ref, ke_ref, ve_ref):
        d = ATT_DILATIONS[g]
        per_res = ATT_UNITS // d
        r = u // per_res
        n = u % per_res
        r0 = pl.multiple_of(u * blk, blk)
        e0 = pl.multiple_of((r * (per_res + 1) + n) * blk, blk)
        q = jnp.concatenate([q_ref[pl.ds(r0, blk), hd * LANES:(hd + 1) * LANES] for hd in range(ATT_REP)], axis=0)
        q_aug = jnp.concatenate([q, eye_ref[...]], axis=1)
        k_aug = jnp.concatenate([ke_ref[pl.ds(e0, 2 * blk), :],
                                 mask_ref[2 * g + jnp.where(n == 0, no_prev_tile, 0)]], axis=1)
        s = lax.dot_general(q_aug, k_aug, (((1,), (1,)), ((), ())), preferred_element_type=F32)
        m_cur = jnp.max(s, axis=-1, keepdims=True)
        p = jnp.exp2(s - m_cur).astype(BF16)
        pv = jnp.dot(p, ve_ref[pl.ds(e0, 2 * blk), :], preferred_element_type=F32)
        acc = pv[:, :ATT_HEAD_DIM]
        l_new = pv[:, ATT_HEAD_DIM:]
        m_new = jnp.broadcast_to(m_cur, (ATT_REP * blk, LANES))
        segs = _state_segments(g, u)

        def load(ref):
            return jnp.concatenate([ref[hd, pl.ds(s0, n_rows), :] for hd in range(ATT_REP)
                                    for (s0, _, n_rows) in segs], axis=0)

        if g > 0:
            m_old, l_old, a_old = load(m_ref), load(l_ref), load(acc_ref)
            m_c = m_new
            m_new = jnp.maximum(m_old, m_c)
            w_old = jnp.exp2(m_old - m_new)
            w_cur = jnp.exp2(m_c - m_new)
            l_new = w_old * l_old + w_cur * l_new
            acc = w_old * a_old + w_cur * acc
        for hd in range(ATT_REP):
            for (s0, v0, n_rows) in segs:
                rows = slice(hd * blk + v0, hd * blk + v0 + n_rows)
                m_ref[hd, pl.ds(s0, n_rows), :] = m_new[rows]
                l_ref[hd, pl.ds(s0, n_rows), :] = l_new[rows]
                acc_ref[hd, pl.ds(s0, n_rows), :] = acc[rows]

    for g in range(len(ATT_DILATIONS)):
        q_ref, _, _, _, _, ke_ref, ve_ref = groups[g]

        def body(u, carry, g=g, q_ref=q_ref, ke_ref=ke_ref, ve_ref=ve_ref):
            unit(u, g, q_ref, ke_ref, ve_ref)
            return carry

        lax.fori_loop(0, ATT_UNITS, body, 0, unroll=ATT_UNROLL)

    def finish(m, carry):
        p0 = pl.multiple_of(m * ATT_MAX_DIL, ATT_MAX_DIL)
        for hd in range(ATT_REP):
            halves = []
            for j in range(ATT_MAX_DIL // SUBLANES):
                idx = pl.ds(j * SUBLANES * ATT_PITCH + m, SUBLANES, stride=ATT_PITCH)
                halves.append(acc_ref[hd, idx, :] / l_ref[hd, idx, :])
            o_ref[pl.ds(p0, ATT_MAX_DIL), hd * LANES:(hd + 1) * LANES] = jnp.concatenate(halves, axis=0).astype(BF16)
        return carry

    lax.fori_loop(0, ATT_BLOCK, finish, 0, unroll=ATT_UNROLL)


def _attn(qs, ks, vs, bsz, seq):
    n_att = seq // ATT_TILE
    cur = lambda b, i, j: (b, i, j)
    prev = lambda b, i, j: (b, jnp.maximum(i - 1, 0), j)
    q_spec = pl.BlockSpec((None, ATT_TILE, ATT_REP * ATT_HEAD_DIM), cur)
    kv_cur = pl.BlockSpec((None, ATT_TILE, ATT_HEAD_DIM), cur)
    kv_prev = pl.BlockSpec((None, ATT_TILE, ATT_HEAD_DIM), prev)
    width = ATT_REP * ATT_HEAD_DIM
    stat = pltpu.VMEM((ATT_REP, ATT_MAX_DIL * ATT_PITCH, ATT_HEAD_DIM), F32)
    return pl.pallas_call(
        _attn_kernel,
        grid=(bsz, n_att, ATT_KV_HEADS),
        in_specs=[q_spec] * 3 + [kv_cur] * 6 + [kv_prev] * 6,
        out_specs=pl.BlockSpec((None, ATT_TILE, width), cur),
        out_shape=jax.ShapeDtypeStruct((bsz, seq, ATT_HEADS * ATT_HEAD_DIM), BF16),
        scratch_shapes=[stat, stat, stat,
                        pltpu.VMEM((2 * len(ATT_DILATIONS), 2 * ATT_BLOCK, ATT_BLOCK), BF16),
                        pltpu.VMEM((ATT_REP * ATT_BLOCK, ATT_BLOCK), BF16)]
        + [pltpu.VMEM((_ext_rows(d), ATT_HEAD_DIM), BF16) for d in ATT_DILATIONS]
        + [pltpu.VMEM((_ext_rows(d), 2 * ATT_HEAD_DIM), BF16) for d in ATT_DILATIONS],
        compiler_params=_params(("parallel", "parallel", "parallel")),
        name="attn",
    )(*qs, *ks, *vs, *ks, *vs)


def _rope_lane_order():
    half = ROPE_DIM // 2
    rest = list(range(ROPE_DIM, ATT_HEAD_DIM))
    n_low = LANES // 2 - half
    return list(range(half)) + rest[:n_low] + list(range(half, ROPE_DIM)) + rest[n_low:]


def _permute_head_dims(w, n_heads):
    order = jnp.asarray(_rope_lane_order())
    cols = (jnp.arange(n_heads)[:, None] * ATT_HEAD_DIM + order[None, :]).reshape(-1)
    return w[:, cols]


def _rope_tables(seq):
    half = ROPE_DIM // 2
    inv_freq = jnp.power(jnp.float32(ROPE_THETA), -jnp.arange(0, ROPE_DIM, 2, dtype=F32) / ROPE_DIM)
    ang = jnp.arange(seq, dtype=jnp.int32).astype(F32)[:, None] * inv_freq[None, :]
    cos, sin = jnp.cos(ang), jnp.sin(ang)
    ones = jnp.ones((seq, LANES // 2 - half), F32)
    zeros = jnp.zeros_like(ones)
    cos_t = jnp.concatenate([cos, ones, cos, ones], axis=1)
    sin_t = jnp.concatenate([-sin, zeros, sin, zeros], axis=1)
    regroup = lambda tbl: tbl.reshape(
        seq // ROW_TILE, QKV_M, ATT_MAX_DIL, LANES).transpose(0, 2, 1, 3).reshape(seq, LANES)
    return regroup(cos_t), regroup(sin_t)


def _head_expand():
    head = jnp.arange(LANES)[:, None]
    chan_head = (jnp.arange(SSM_D_INNER) // SSM_HEAD_DIM)[None, :]
    e = (head == chan_head).astype(BF16)
    return jnp.concatenate([e, e], axis=0)


def _dt_weights(w):
    w = jnp.pad(w, ((0, 0), (0, LANES - SSM_N_HEADS)))
    hi = w.astype(BF16)
    lo = (w - hi.astype(F32)).astype(BF16)
    return jnp.concatenate([hi, hi, lo], axis=0)


def kernel(x, a_norm, ssm_w_in, ssm_conv_w, ssm_conv_b, ssm_dt_bias, ssm_a_log, ssm_d, ssm_norm, ssm_w_out,
           kv_norm, w_kv, b_norm, att_w_q, att_w_o, ffn_norm, ffn_w_up, ffn_conv_w, ffn_w_down, final_norm):
    bsz, seq, dm = x.shape
    assert dm == D_MODEL and seq % ATT_TILE == 0
    assert a_norm.shape[0] == 1 and b_norm.shape[0] == 1 and ffn_norm.shape[0] == 2
    t = bsz * seq
    x2 = x.reshape(t, dm)
    row = lambda v: v.reshape(1, -1).astype(F32)

    w_in = ssm_w_in[0]
    wz = w_in[:, :SSM_D_INNER].astype(BF16)
    wx = w_in[:, SSM_D_INNER:SSM_D_INNER + SSM_CONV_DIM].astype(BF16)
    wdt3 = _dt_weights(w_in[:, SSM_D_INNER + SSM_CONV_DIM:])
    pad_heads = lambda v: jnp.pad(row(v), ((0, 0), (0, LANES - SSM_N_HEADS)))
    z, xbc, dt = _in_proj(x2, row(a_norm[0]), wz, wx, wdt3, ssm_conv_w[0], row(ssm_conv_b[0]), bsz, seq)
    y = _ssd(xbc, z, dt, pad_heads(ssm_dt_bias[0]), pad_heads(ssm_a_log[0]),
             row(jnp.repeat(ssm_d[0], SSM_HEAD_DIM)), row(ssm_norm[0]), _head_expand(), bsz, seq)
    x2 = _ffn(x2, y, ssm_w_out[0].astype(BF16), row(ffn_norm[0]), ffn_w_up[0].astype(BF16), ffn_conv_w[0],
              ffn_w_down[0].astype(BF16), row(final_norm), bsz, seq, False)

    cos_t, sin_t = _rope_tables(seq)
    n_q_heads = len(ATT_DILATIONS) * ATT_HEADS
    n_k_heads = len(ATT_DILATIONS) * ATT_KV_HEADS
    k_cols = n_k_heads * ATT_HEAD_DIM
    wq = _permute_head_dims(att_w_q[0], n_q_heads).astype(BF16)
    wkv = jnp.concatenate([_permute_head_dims(w_kv[:, :k_cols], n_k_heads), w_kv[:, k_cols:]], axis=1).astype(BF16)
    outs = _qkv(x2, row(kv_norm), row(b_norm[0]), wq, wkv, cos_t, sin_t, bsz, seq)
    o = _attn(outs[0:3], outs[3:6], outs[6:9], bsz, seq)
    x2 = _ffn(x2, o.reshape(t, dm), att_w_o[0].astype(BF16), row(ffn_norm[1]), ffn_w_up[1].astype(BF16),
              ffn_conv_w[1], ffn_w_down[1].astype(BF16), row(final_norm), bsz, seq, True)
    return x2.reshape(bsz, seq, dm)
```

```python
import functools
import math

import jax
import jax.numpy as jnp
from jax import lax
from jax.experimental import pallas as pl
from jax.experimental.pallas import tpu as pltpu

F32 = jnp.float32
BF16 = jnp.bfloat16

D_MODEL = 1024
RMS_EPS = 1e-6
GATED_NORM_EPS = 1e-5

SSM_D_INNER = 2048
SSM_HEAD_DIM = 64
SSM_N_HEADS = 32
SSM_N_GROUPS = 8
SSM_HEADS_PER_GROUP = SSM_N_HEADS // SSM_N_GROUPS
SSM_GROUP_CH = SSM_D_INNER // SSM_N_GROUPS
SSM_D_STATE = 128
SSM_CONV = 4
SSM_CHUNK = 128
SSM_BC_DIM = SSM_N_GROUPS * SSM_D_STATE
SSM_CONV_DIM = SSM_D_INNER + 2 * SSM_BC_DIM

ATT_DILATIONS = (1, 4, 16)
ATT_MAX_DIL = max(ATT_DILATIONS)
ATT_BLOCK = 128
ATT_HEAD_DIM = 128
ATT_HEADS = 8
ATT_KV_HEADS = 2
ATT_REP = ATT_HEADS // ATT_KV_HEADS
ATT_TILE = ATT_BLOCK * ATT_MAX_DIL
ATT_UNITS = ATT_TILE // ATT_BLOCK
ATT_Q_GROUP_DIM = ATT_HEADS * ATT_HEAD_DIM
ATT_KV_GROUP_DIM = ATT_KV_HEADS * ATT_HEAD_DIM
ROPE_DIM = 32
ROPE_THETA = 500000.0

FFN_DIM = 2816
FFN_CONV = 3
FFN_COL = 256

LANES = 128
SUBLANES = 8
BF16_ROWS = 2 * SUBLANES
NEG_BIG = -1e30
VMEM_LIMIT = 56 * 1024 * 1024

ROW_TILE = 512
ROW_GROUPS = ROW_TILE // SUBLANES


def _const_spec(shape):
    nd = len(shape)
    return pl.BlockSpec(shape, lambda *_: (0,) * nd, pipeline_mode=pl.Buffered(1))


def _params(sem):
    return pltpu.CompilerParams(dimension_semantics=sem, vmem_limit_bytes=VMEM_LIMIT)


def _silu(v):
    return v / (1.0 + jnp.exp2(v * (-math.log2(math.e))))


def _log1p(v):
    u = 1.0 + v
    return jnp.where(u == 1.0, v, jnp.log(u) * (v / (u - 1.0)))


def _rms_normed(x, w):
    return x * lax.rsqrt(jnp.mean(x * x, axis=-1, keepdims=True) + RMS_EPS) * w


def _pipelined(jobs):
    pending = jobs[0][0]()
    for i, (_, epilogue) in enumerate(jobs):
        upcoming = jobs[i + 1][0]() if i + 1 < len(jobs) else None
        epilogue(pending)
        pending = upcoming


def _permute_rows(v, scr_ref):
    n_ct = v.shape[1] // LANES
    for ct in range(n_ct):
        for s in range(SUBLANES):
            scr_ref[ct, pl.ds(s, ROW_GROUPS, stride=SUBLANES), :] = (
                v[s * ROW_GROUPS:(s + 1) * ROW_GROUPS, ct * LANES:(ct + 1) * LANES])
    return jnp.concatenate([scr_ref[ct] for ct in range(n_ct)], axis=1)


def _unpermute_rows(v, scr_ref, emit):
    n_ct = v.shape[1] // LANES
    for ct in range(n_ct):
        scr_ref[ct] = v[:, ct * LANES:(ct + 1) * LANES]
    for ct in range(n_ct):
        for s in range(SUBLANES):
            emit(s, ct, scr_ref[ct, pl.ds(s, ROW_GROUPS, stride=SUBLANES), :])


CONV_STRIP = 32


def _conv_halo(u, prev_tail, n_taps):
    halo = n_taps - 1
    tail = u[ROW_TILE - halo * SUBLANES:, :]
    sub = lax.broadcasted_iota(jnp.int32, (SUBLANES, u.shape[1]), 0)
    fix = []
    for i in range(halo):
        rows = slice(i * SUBLANES, (i + 1) * SUBLANES)
        fix.append(jnp.where(sub == 0, pltpu.roll(prev_tail[rows], 1, axis=0), pltpu.roll(tail[rows], 1, axis=0)))
    return jnp.concatenate(fix, axis=0), tail


def _conv_strip(u, fix, taps, n_taps, r0):
    halo = n_taps - 1
    out = taps[halo] * u[r0:r0 + CONV_STRIP]
    for k in range(1, halo + 1):
        start = r0 - k * SUBLANES
        if start >= 0:
            shifted = u[start:start + CONV_STRIP]
        else:
            shifted = jnp.concatenate([fix[halo * SUBLANES + start:], u[:start + CONV_STRIP]], axis=0)
        out = out + taps[halo - k] * shifted
    return out


IN_NC = 512
SSM_HALO_ROWS = (SSM_CONV - 1) * SUBLANES


def _in_proj_kernel(x_ref, nw_ref, wz_ref, wx_ref, wdt_ref, cw_ref, cb_ref, z_ref, xbc_ref, dt_ref,
                    perm_ref, out_ref, tail_ref):
    @pl.when(pl.program_id(1) == 0)
    def _():
        tail_ref[...] = jnp.zeros_like(tail_ref)

    h = _rms_normed(x_ref[...], nw_ref[...])
    hb = h.astype(BF16)
    hp = _permute_rows(h, perm_ref).astype(BF16)

    n_x = SSM_CONV_DIM // IN_NC
    z_nc = SSM_D_INNER // n_x

    def chunk_job(i):
        cols = slice(i * IN_NC, (i + 1) * IN_NC)
        zcols = slice(i * z_nc, (i + 1) * z_nc)

        def matmul():
            return (jnp.dot(hp, wx_ref[:, cols], preferred_element_type=F32),
                    jnp.dot(hb, wz_ref[:, zcols], preferred_element_type=F32))

        def epilogue(res):
            u, uz = res
            z_ref[:, zcols] = uz.astype(BF16)
            taps = [cw_ref[k:k + 1, cols] for k in range(SSM_CONV)]
            bias = cb_ref[:, cols]
            fix, tail = _conv_halo(u, tail_ref[:, cols], SSM_CONV)
            tail_ref[:, cols] = tail
            stage = out_ref.at[i % 2]
            for r0 in range(0, ROW_TILE, CONV_STRIP):
                act = _silu(_conv_strip(u, fix, taps, SSM_CONV, r0) + bias)
                for ct in range(IN_NC // LANES):
                    stage[ct, r0:r0 + CONV_STRIP, :] = act[:, ct * LANES:(ct + 1) * LANES]
            for ct in range(IN_NC // LANES):
                c0 = i * IN_NC + ct * LANES
                for s in range(SUBLANES):
                    xbc_ref[s * ROW_GROUPS:(s + 1) * ROW_GROUPS, c0:c0 + LANES] = (
                        stage[ct, pl.ds(s, ROW_GROUPS, stride=SUBLANES), :].astype(BF16))

        return matmul, epilogue

    def dt_job():
        def matmul():
            h_lo = (h - hb.astype(F32)).astype(BF16)
            return jnp.dot(jnp.concatenate([hb, h_lo, hb], axis=1), wdt_ref[...], preferred_element_type=F32)

        def epilogue(u):
            dt_ref[...] = u

        return matmul, epilogue

    _pipelined([chunk_job(i) for i in range(n_x)] + [dt_job()])


def _in_proj(x2, nw, wz, wx, wdt3, cw, cb, bsz, seq):
    nt = seq // ROW_TILE
    rows = lambda b, s: (b * nt + s, 0)
    t = bsz * seq
    return pl.pallas_call(
        _in_proj_kernel,
        grid=(bsz, nt),
        in_specs=[
            pl.BlockSpec((ROW_TILE, D_MODEL), rows),
            _const_spec((1, D_MODEL)),
            _const_spec((D_MODEL, SSM_D_INNER)),
            _const_spec((D_MODEL, SSM_CONV_DIM)),
            _const_spec((3 * D_MODEL, LANES)),
            _const_spec((SSM_CONV, SSM_CONV_DIM)),
            _const_spec((1, SSM_CONV_DIM)),
        ],
        out_specs=[
            pl.BlockSpec((ROW_TILE, SSM_D_INNER), rows),
            pl.BlockSpec((ROW_TILE, SSM_CONV_DIM), rows),
            pl.BlockSpec((ROW_TILE, LANES), rows),
        ],
        out_shape=[
            jax.ShapeDtypeStruct((t, SSM_D_INNER), BF16),
            jax.ShapeDtypeStruct((t, SSM_CONV_DIM), BF16),
            jax.ShapeDtypeStruct((t, LANES), F32),
        ],
        scratch_shapes=[
            pltpu.VMEM((D_MODEL // LANES, ROW_TILE, LANES), F32),
            pltpu.VMEM((2, IN_NC // LANES, ROW_TILE, LANES), F32),
            pltpu.VMEM((SSM_HALO_ROWS, SSM_CONV_DIM), F32),
        ],
        compiler_params=_params(("parallel", "arbitrary")),
        name="in_proj",
    )(x2, nw, wz, wx, wdt3, cw, cb)


def _split_hi_lo(v):
    hi = v.astype(BF16)
    lo = (v - hi.astype(F32)).astype(BF16)
    return jnp.concatenate([hi, lo], axis=1)


SSD_CHUNKS_PER_STEP = 2


def _ssd_kernel(xbc_ref, z_ref, dt_ref, x_ref, dtb_ref, alog_ref, dsk_ref, nw_ref, exp_ref, wout_ref, o_ref,
                y_ref, state_ref):
    L = SSM_CHUNK
    P = SSM_HEAD_DIM

    @pl.when(pl.program_id(1) == 0)
    def _():
        state_ref[...] = jnp.zeros_like(state_ref)

    row = lax.broadcasted_iota(jnp.int32, (L, L), 0)
    col = lax.broadcasted_iota(jnp.int32, (L, L), 1)
    causal = row >= col
    tril = jnp.where(causal, 1.0, 0.0).astype(BF16)
    lane_head = lax.broadcasted_iota(jnp.int32, (L, SSM_GROUP_CH), 1) // P
    head_mask = [jnp.where(lane_head == hh, 1.0, 0.0).astype(BF16) for hh in range(SSM_HEADS_PER_GROUP)]
    neg_a = -jnp.exp(alog_ref[...])

    def decays(rows):
        dt_raw = dt_ref[rows, :] + dtb_ref[...]
        dt = jnp.maximum(dt_raw, 0.0) + _log1p(jnp.exp(-jnp.abs(dt_raw)))
        adt = dt * neg_a
        a_hi = adt.astype(BF16)
        a_r1 = adt - a_hi.astype(F32)
        a_mid = a_r1.astype(BF16)
        a_lo = (a_r1 - a_mid.astype(F32)).astype(BF16)
        a_cs = (jnp.dot(tril, a_hi, preferred_element_type=F32) + jnp.dot(tril, a_mid, preferred_element_type=F32)
                + jnp.dot(tril, a_lo, preferred_element_type=F32))
        a_last = a_cs[L - 1:L, :]
        w_state = dt * jnp.exp(a_last - a_cs)
        e_acs = jnp.exp(a_cs)
        e_last = jnp.broadcast_to(jnp.exp(a_last), (BF16_ROWS, LANES))
        expanded = jnp.dot(_split_hi_lo(jnp.concatenate([dt, w_state, e_acs, e_last], axis=0)), exp_ref[...],
                           preferred_element_type=F32)
        return a_cs, a_cs.T, expanded

    def scan(rows, a_cs, a_cs_t, expanded):
        for g in range(SSM_N_GROUPS):
            xcols = slice(g * SSM_GROUP_CH, (g + 1) * SSM_GROUP_CH)
            x_g = xbc_ref[rows, xcols].astype(F32)
            b_bf = xbc_ref[rows, SSM_D_INNER + g * SSM_D_STATE:SSM_D_INNER + (g + 1) * SSM_D_STATE]
            c_bf = xbc_ref[rows, SSM_D_INNER + SSM_BC_DIM + g * SSM_D_STATE:
                           SSM_D_INNER + SSM_BC_DIM + (g + 1) * SSM_D_STATE]
            cb = lax.dot_general(c_bf, b_bf, (((1,), (1,)), ((), ())), preferred_element_type=F32)
            prev = state_ref[g]
            y_off = jnp.dot(c_bf, prev.astype(BF16), preferred_element_type=F32)
            xdt = (x_g * expanded[0:L, xcols]).astype(BF16)
            xw = (x_g * expanded[L:2 * L, xcols]).astype(BF16)

            lhs, rhs = [], []
            for hh in range(SSM_HEADS_PER_GROUP):
                h = g * SSM_HEADS_PER_GROUP + hh
                seg = jnp.broadcast_to(a_cs[:, h:h + 1], (L, L)) - a_cs_t[h:h + 1, :]
                lhs.append((cb * jnp.exp(jnp.where(causal, seg, NEG_BIG))).astype(BF16))
                rhs.append(xdt * head_mask[hh])
            y_g = jnp.dot(jnp.concatenate(lhs, axis=1), jnp.concatenate(rhs, axis=0), preferred_element_type=F32)
            y_g = y_g + y_off * expanded[2 * L:3 * L, xcols] + x_g * dsk_ref[:, xcols]

            new_state = jnp.dot(b_bf.T, xw, preferred_element_type=F32)
            state_ref[g] = prev * expanded[3 * L:3 * L + 1, xcols] + new_state

            y_g = y_g * _silu(z_ref[rows, xcols].astype(F32))
            y_g = y_g * lax.rsqrt(jnp.mean(y_g * y_g, axis=-1, keepdims=True) + GATED_NORM_EPS)
            y_ref[rows, xcols] = (y_g * nw_ref[:, xcols]).astype(BF16)

    chunks = [slice(ci * L, (ci + 1) * L) for ci in range(SSD_CHUNKS_PER_STEP)]
    pre = [decays(rows) for rows in chunks]
    for rows, args in zip(chunks, pre):
        scan(rows, *args)
        o_ref[rows, :] = x_ref[rows, :] + jnp.dot(y_ref[rows, :], wout_ref[...], preferred_element_type=F32)


def _ssd(xbc, z, dt, x2, dtb, alog, dsk, nw, expand, wout, bsz, seq):
    L = SSD_CHUNKS_PER_STEP * SSM_CHUNK
    nc = seq // L
    rows = lambda b, c: (b * nc + c, 0)
    return pl.pallas_call(
        _ssd_kernel,
        grid=(bsz, nc),
        in_specs=[
            pl.BlockSpec((L, SSM_CONV_DIM), rows),
            pl.BlockSpec((L, SSM_D_INNER), rows),
            pl.BlockSpec((L, LANES), rows),
            pl.BlockSpec((L, D_MODEL), rows),
            _const_spec((1, LANES)),
            _const_spec((1, LANES)),
            _const_spec((1, SSM_D_INNER)),
            _const_spec((1, SSM_D_INNER)),
            _const_spec((2 * LANES, SSM_D_INNER)),
            _const_spec((SSM_D_INNER, D_MODEL)),
        ],
        out_specs=pl.BlockSpec((L, D_MODEL), rows),
        out_shape=jax.ShapeDtypeStruct((bsz * seq, D_MODEL), F32),
        scratch_shapes=[pltpu.VMEM((L, SSM_D_INNER), BF16),
                        pltpu.VMEM((SSM_N_GROUPS, SSM_D_STATE, SSM_GROUP_CH), F32)],
        compiler_params=_params(("parallel", "arbitrary")),
        name="ssd",
    )(xbc, z, dt, x2, dtb, alog, dsk, nw, expand, wout)


FFN_HALO_ROWS = (FFN_CONV - 1) * SUBLANES


def _ffn_kernel(*refs, has_pre, final_norm):
    if has_pre:
        x_ref, pre_ref, wpre_ref, nw_ref, wup_ref, cw_ref, wdn_ref, fnw_ref, o_ref, perm_ref, tail_ref = refs
    else:
        x_ref, nw_ref, wup_ref, cw_ref, wdn_ref, fnw_ref, o_ref, perm_ref, tail_ref = refs

    @pl.when(pl.program_id(1) == 0)
    def _():
        tail_ref[...] = jnp.zeros_like(tail_ref)

    x1 = x_ref[...]
    if has_pre:
        x1 = x1 + jnp.dot(pre_ref[...], wpre_ref[...], preferred_element_type=F32)
    hp = _permute_rows(_rms_normed(x1, nw_ref[...]), perm_ref).astype(BF16)
    acc = [None]

    def chunk_job(c):
        def matmul():
            return [jnp.dot(hp, wup_ref[:, half * FFN_DIM + c * FFN_COL:half * FFN_DIM + (c + 1) * FFN_COL],
                            preferred_element_type=F32) for half in range(2)]

        def epilogue(us):
            taps, fixes = [], []
            for half, u in enumerate(us):
                cols = slice(half * FFN_DIM + c * FFN_COL, half * FFN_DIM + (c + 1) * FFN_COL)
                taps.append([cw_ref[k:k + 1, cols] for k in range(FFN_CONV)])
                fix, tail = _conv_halo(u, tail_ref[:, cols], FFN_CONV)
                tail_ref[:, cols] = tail
                fixes.append(fix)
            strips = []
            for r0 in range(0, ROW_TILE, CONV_STRIP):
                gate, val = [_conv_strip(us[half], fixes[half], taps[half], FFN_CONV, r0) for half in range(2)]
                strips.append((_silu(gate) * val).astype(BF16))
            act = jnp.concatenate(strips, axis=0)
            part = jnp.dot(act, wdn_ref[c * FFN_COL:(c + 1) * FFN_COL, :], preferred_element_type=F32)
            acc[0] = part if acc[0] is None else acc[0] + part

        return matmul, epilogue

    _pipelined([chunk_job(c) for c in range(FFN_DIM // FFN_COL)])

    def emit(s, ct, block):
        o_ref[s * ROW_GROUPS:(s + 1) * ROW_GROUPS, ct * LANES:(ct + 1) * LANES] = block

    _unpermute_rows(acc[0], perm_ref, emit)
    out = x1 + o_ref[...]
    if final_norm:
        out = _rms_normed(out, fnw_ref[...])
    o_ref[...] = out


def _ffn(x2, pre, wpre, nw, wup, cw, wdn, fnw, bsz, seq, final_norm):
    nt = seq // ROW_TILE
    rows = lambda b, s: (b * nt + s, 0)
    has_pre = pre is not None
    pre_specs = [pl.BlockSpec((ROW_TILE, pre.shape[1]), rows), _const_spec(wpre.shape)] if has_pre else []
    pre_args = (pre, wpre) if has_pre else ()
    return pl.pallas_call(
        functools.partial(_ffn_kernel, has_pre=has_pre, final_norm=final_norm),
        grid=(bsz, nt),
        in_specs=[pl.BlockSpec((ROW_TILE, D_MODEL), rows)] + pre_specs + [
            _const_spec((1, D_MODEL)),
            _const_spec((D_MODEL, 2 * FFN_DIM)),
            _const_spec((FFN_CONV, 2 * FFN_DIM)),
            _const_spec((FFN_DIM, D_MODEL)),
            _const_spec((1, D_MODEL)),
        ],
        out_specs=pl.BlockSpec((ROW_TILE, D_MODEL), rows),
        out_shape=jax.ShapeDtypeStruct((bsz * seq, D_MODEL), F32),
        scratch_shapes=[
            pltpu.VMEM((D_MODEL // LANES, ROW_TILE, LANES), F32),
            pltpu.VMEM((FFN_HALO_ROWS, 2 * FFN_DIM), F32),
        ],
        compiler_params=_params(("parallel", "arbitrary")),
        name="ffn_final" if final_norm else "ffn",
    )(x2, *pre_args, nw, wup, cw, wdn, fnw)


QKV_SUB = ATT_TILE // ROW_TILE
QKV_M = ROW_TILE // ATT_MAX_DIL


def _rotary(v, cos, sin):
    return v * cos + pltpu.roll(v, LANES // 2, axis=1) * sin


def _qkv_kernel(x_ref, kvn_ref, qn_ref, wq_ref, wkv_ref, cos_ref, sin_ref,
                q0_ref, q1_ref, q2_ref, k0_ref, k1_ref, k2_ref, v0_ref, v1_ref, v2_ref, scr_ref):
    tm = ROW_TILE
    for ct in range(D_MODEL // LANES):
        scr_ref[ct] = x_ref[:, ct * LANES:(ct + 1) * LANES]
    x = jnp.concatenate(
        [jnp.concatenate([scr_ref[ct, pl.ds(c, QKV_M, stride=ATT_MAX_DIL), :] for c in range(ATT_MAX_DIL)], axis=0)
         for ct in range(D_MODEL // LANES)], axis=1)
    xn = x * lax.rsqrt(jnp.mean(x * x, axis=-1, keepdims=True) + RMS_EPS)
    h_kv = (xn * kvn_ref[...]).astype(BF16)
    h_q = (xn * qn_ref[...]).astype(BF16)
    cos = cos_ref[...]
    sin = sin_ref[...]
    q_scale = ATT_HEAD_DIM ** -0.5 * math.log2(math.e)

    def emit(res, out_ref, d):
        by_c = [res[c * QKV_M:(c + 1) * QKV_M] for c in range(ATT_MAX_DIL)]
        if d == 16:
            for c in range(ATT_MAX_DIL):
                out_ref[c] = by_c[c].astype(BF16)
        elif d == 4:
            for c in range(ATT_MAX_DIL):
                out_ref[c % 4, c // 4] = by_c[c].astype(BF16)
        else:
            for nl in range(tm // ATT_BLOCK):
                ks = slice(nl * SUBLANES, (nl + 1) * SUBLANES)
                for c in range(0, ATT_MAX_DIL, 2):
                    r0 = nl * ATT_BLOCK + c * SUBLANES
                    out_ref[r0:r0 + BF16_ROWS, :] = jnp.concatenate(
                        [by_c[c][ks], by_c[c + 1][ks]], axis=0).astype(BF16)

    def rot(v, n_heads):
        return jnp.concatenate(
            [_rotary(v[:, hd * LANES:(hd + 1) * LANES], cos, sin) for hd in range(n_heads)], axis=1)

    q_refs = (q0_ref, q1_ref, q2_ref)
    k_refs = (k0_ref, k1_ref, k2_ref)
    v_refs = (v0_ref, v1_ref, v2_ref)
    jobs = []
    for g, d in enumerate(ATT_DILATIONS):
        qc = slice(g * ATT_Q_GROUP_DIM, (g + 1) * ATT_Q_GROUP_DIM)
        kc = slice(g * ATT_KV_GROUP_DIM, (g + 1) * ATT_KV_GROUP_DIM)
        v0 = len(ATT_DILATIONS) * ATT_KV_GROUP_DIM + g * ATT_KV_GROUP_DIM
        vc = slice(v0, v0 + ATT_KV_GROUP_DIM)
        jobs.append((lambda qc=qc: jnp.dot(h_q, wq_ref[:, qc], preferred_element_type=F32),
                     lambda u, g=g, d=d: emit(rot(u, ATT_HEADS) * q_scale, q_refs[g], d)))
        jobs.append((lambda kc=kc: jnp.dot(h_kv, wkv_ref[:, kc], preferred_element_type=F32),
                     lambda u, g=g, d=d: emit(rot(u, ATT_KV_HEADS), k_refs[g], d)))
        jobs.append((lambda vc=vc: jnp.dot(h_kv, wkv_ref[:, vc], preferred_element_type=F32),
                     lambda u, g=g, d=d: emit(u, v_refs[g], d)))
    _pipelined(jobs)


def _qkv(x2, kvn, qn, wq, wkv, cos_t, sin_t, bsz, seq):
    tm = ROW_TILE
    nt = seq // tm
    n_att = seq // ATT_TILE

    def out_arrays(cols):
        shapes, specs = [], []
        for d in ATT_DILATIONS:
            if d == 1:
                shapes.append(jax.ShapeDtypeStruct((bsz * seq, cols), BF16))
                specs.append(pl.BlockSpec((tm, cols), lambda b, s: (b * nt + s, 0)))
            elif d == 4:
                shapes.append(jax.ShapeDtypeStruct((bsz, n_att, 4, QKV_SUB, 4, QKV_M, cols), BF16))
                specs.append(pl.BlockSpec((None, None, 4, None, 4, QKV_M, cols),
                                          lambda b, s: (b, s // QKV_SUB, 0, s % QKV_SUB, 0, 0, 0)))
            else:
                shapes.append(jax.ShapeDtypeStruct((bsz, n_att, d, QKV_SUB, QKV_M, cols), BF16))
                specs.append(pl.BlockSpec((None, None, d, None, QKV_M, cols),
                                          lambda b, s: (b, s // QKV_SUB, 0, s % QKV_SUB, 0, 0)))
        return shapes, specs

    q_shapes, q_specs = out_arrays(ATT_Q_GROUP_DIM)
    k_shapes, k_specs = out_arrays(ATT_KV_GROUP_DIM)
    v_shapes, v_specs = out_arrays(ATT_KV_GROUP_DIM)
    outs = pl.pallas_call(
        _qkv_kernel,
        grid=(bsz, nt),
        in_specs=[
            pl.BlockSpec((tm, D_MODEL), lambda b, s: (b * nt + s, 0)),
            _const_spec((1, D_MODEL)),
            _const_spec((1, D_MODEL)),
            _const_spec(wq.shape),
            _const_spec(wkv.shape),
            pl.BlockSpec((tm, LANES), lambda b, s: (s, 0)),
            pl.BlockSpec((tm, LANES), lambda b, s: (s, 0)),
        ],
        out_specs=q_specs + k_specs + v_specs,
        out_shape=q_shapes + k_shapes + v_shapes,
        scratch_shapes=[pltpu.VMEM((ATT_Q_GROUP_DIM // LANES, tm, LANES), F32)],
        compiler_params=_params(("parallel", "parallel")),
        name="qkv",
    )(x2, kvn, qn, wq, wkv, cos_t, sin_t)
    return [o.reshape(bsz, seq, o.shape[-1]) for o in outs]


ATT_PITCH = ATT_BLOCK + SUBLANES
ATT_UNROLL = ATT_UNITS


def _ext_rows(d):
    return (ATT_UNITS // d + 1) * d * ATT_BLOCK


def _unit_order(g, rho):
    d = ATT_DILATIONS[g]
    if d == 1:
        return (rho % SUBLANES) * ATT_MAX_DIL + rho // SUBLANES
    if d == 4:
        return (rho % QKV_M) * 4 + rho // QKV_M
    return rho


def _state_segments(g, u):
    d = ATT_DILATIONS[g]
    if d == 1:
        return [(pl.multiple_of(c * ATT_PITCH + u * SUBLANES, SUBLANES), c * SUBLANES, SUBLANES)
                for c in range(ATT_MAX_DIL)]
    if d == 4:
        r4, n4 = u // 4, u % 4
        return [(pl.multiple_of((4 * a + r4) * ATT_PITCH + n4 * QKV_M, SUBLANES), a * QKV_M, QKV_M)
                for a in range(4)]
    return [(pl.multiple_of(u * ATT_PITCH, SUBLANES), 0, ATT_BLOCK)]


def _attn_kernel(q0_ref, q1_ref, q2_ref,
                 k0c_ref, k1c_ref, k2c_ref, v0c_ref, v1c_ref, v2c_ref,
                 k0p_ref, k1p_ref, k2p_ref, v0p_ref, v1p_ref, v2p_ref,
                 o_ref, acc_ref, m_ref, l_ref, mask_ref, eye_ref,
                 ke0_ref, ke1_ref, ke2_ref, ve0_ref, ve1_ref, ve2_ref):
    blk = ATT_BLOCK
    no_prev_tile = (pl.program_id(1) == 0).astype(jnp.int32)

    key_row = lax.broadcasted_iota(jnp.int32, (2 * blk, blk), 0)
    qry_row = lax.broadcasted_iota(jnp.int32, (2 * blk, blk), 1)
    in_prev = key_row < blk
    for g in range(len(ATT_DILATIONS)):
        i_s = _unit_order(g, key_row % blk)
        i_q = _unit_order(g, qry_row)
        cur_ok = (~in_prev) & (i_s <= i_q)
        mask_ref[2 * g] = jnp.where((in_prev & (i_s >= i_q)) | cur_ok, 0.0, NEG_BIG).astype(BF16)
        mask_ref[2 * g + 1] = jnp.where(cur_ok, 0.0, NEG_BIG).astype(BF16)
    e_row = lax.broadcasted_iota(jnp.int32, (ATT_REP * blk, blk), 0)
    e_col = lax.broadcasted_iota(jnp.int32, (ATT_REP * blk, blk), 1)
    eye_ref[...] = jnp.where(e_row % blk == e_col, 1.0, 0.0).astype(BF16)

    groups = ((q0_ref, k0c_ref, v0c_ref, k0p_ref, v0p_ref, ke0_ref, ve0_ref),
              (q1_ref, k1c_ref, v1c_ref, k1p_ref, v1p_ref, ke1_ref, ve1_ref),
              (q2_ref, k2c_ref, v2c_ref, k2p_ref, v2p_ref, ke2_ref, ve2_ref))

    for g, d in enumerate(ATT_DILATIONS):
        _, kc_ref, vc_ref, kp_ref, vp_ref, ke_ref, ve_ref = groups[g]
        per_res = ATT_UNITS // d
        ve_ref[:, ATT_HEAD_DIM:] = jnp.ones((_ext_rows(d), ATT_HEAD_DIM), BF16)
        for r in range(d):
            dst = r * (per_res + 1) * blk
            src_prev = (r * per_res + per_res - 1) * blk
            src = r * per_res * blk
            ke_ref[dst:dst + blk, :] = kp_ref[src_prev:src_prev + blk, :]
            ve_ref[dst:dst + blk, 0:ATT_HEAD_DIM] = vp_ref[src_prev:src_prev + blk, :]
            ke_ref[dst + blk:dst + (per_res + 1) * blk, :] = kc_ref[src:src + per_res * blk, :]
            ve_ref[dst + blk:dst + (per_res + 1) * blk, 0:ATT_HEAD_DIM] = vc_ref[src:src + per_res * blk, :]

    def unit(u, g, q_ref, ke_ref, ve_ref):
        d = ATT_DILATIONS[g]
        per_res = ATT_UNITS // d
        r = u // per_res
        n = u % per_res
        r0 = pl.multiple_of(u * blk, blk)
        e0 = pl.multiple_of((r * (per_res + 1) + n) * blk, blk)
        q = jnp.concatenate([q_ref[pl.ds(r0, blk), hd * LANES:(hd + 1) * LANES] for hd in range(ATT_REP)], axis=0)
        q_aug = jnp.concatenate([q, eye_ref[...]], axis=1)
        k_aug = jnp.concatenate([ke_ref[pl.ds(e0, 2 * blk), :],
                                 mask_ref[2 * g + jnp.where(n == 0, no_prev_tile, 0)]], axis=1)
        s = lax.dot_general(q_aug, k_aug, (((1,), (1,)), ((), ())), preferred_element_type=F32)
        m_cur = jnp.max(s, axis=-1, keepdims=True)
        p = jnp.exp2(s - m_cur).astype(BF16)
        pv = jnp.dot(p, ve_ref[pl.ds(e0, 2 * blk), :], preferred_element_type=F32)
        acc = pv[:, :ATT_HEAD_DIM]
        l_new = pv[:, ATT_HEAD_DIM:]
        m_new = jnp.broadcast_to(m_cur, (ATT_REP * blk, LANES))
        segs = _state_segments(g, u)

        def load(ref):
            return jnp.concatenate([ref[hd, pl.ds(s0, n_rows), :] for hd in range(ATT_REP)
                                    for (s0, _, n_rows) in segs], axis=0)

        if g > 0:
            m_old, l_old, a_old = load(m_ref), load(l_ref), load(acc_ref)
            m_c = m_new
            m_new = jnp.maximum(m_old, m_c)
            w_old = jnp.exp2(m_old - m_new)
            w_cur = jnp.exp2(m_c - m_new)
            l_new = w_old * l_old + w_cur * l_new
            acc = w_old * a_old + w_cur * acc
        for hd in range(ATT_REP):
            for (s0, v0, n_rows) in segs:
                rows = slice(hd * blk + v0, hd * blk + v0 + n_rows)
                m_ref[hd, pl.ds(s0, n_rows), :] = m_new[rows]
                l_ref[hd, pl.ds(s0, n_rows), :] = l_new[rows]
                acc_ref[hd, pl.ds(s0, n_rows), :] = acc[rows]

    for g in range(len(ATT_DILATIONS)):
        q_ref, _, _, _, _, ke_ref, ve_ref = groups[g]

        def body(u, carry, g=g, q_ref=q_ref, ke_ref=ke_ref, ve_ref=ve_ref):
            unit(u, g, q_ref, ke_ref, ve_ref)
            return carry

        lax.fori_loop(0, ATT_UNITS, body, 0, unroll=ATT_UNROLL)

    def finish(m, carry):
        p0 = pl.multiple_of(m * ATT_MAX_DIL, ATT_MAX_DIL)
        for hd in range(ATT_REP):
            halves = []
            for j in range(ATT_MAX_DIL // SUBLANES):
                idx = pl.ds(j * SUBLANES * ATT_PITCH + m, SUBLANES, stride=ATT_PITCH)
                halves.append(acc_ref[hd, idx, :] / l_ref[hd, idx, :])
            o_ref[pl.ds(p0, ATT_MAX_DIL), hd * LANES:(hd + 1) * LANES] = jnp.concatenate(halves, axis=0).astype(BF16)
        return carry

    lax.fori_loop(0, ATT_BLOCK, finish, 0, unroll=ATT_UNROLL)


def _attn(qs, ks, vs, bsz, seq):
    n_att = seq // ATT_TILE
    cur = lambda b, i, j: (b, i, j)
    prev = lambda b, i, j: (b, jnp.maximum(i - 1, 0), j)
    q_spec = pl.BlockSpec((None, ATT_TILE, ATT_REP * ATT_HEAD_DIM), cur)
    kv_cur = pl.BlockSpec((None, ATT_TILE, ATT_HEAD_DIM), cur)
    kv_prev = pl.BlockSpec((None, ATT_TILE, ATT_HEAD_DIM), prev)
    width = ATT_REP * ATT_HEAD_DIM
    stat = pltpu.VMEM((ATT_REP, ATT_MAX_DIL * ATT_PITCH, ATT_HEAD_DIM), F32)
    return pl.pallas_call(
        _attn_kernel,
        grid=(bsz, n_att, ATT_KV_HEADS),
        in_specs=[q_spec] * 3 + [kv_cur] * 6 + [kv_prev] * 6,
        out_specs=pl.BlockSpec((None, ATT_TILE, width), cur),
        out_shape=jax.ShapeDtypeStruct((bsz, seq, ATT_HEADS * ATT_HEAD_DIM), BF16),
        scratch_shapes=[stat, stat, stat,
                        pltpu.VMEM((2 * len(ATT_DILATIONS), 2 * ATT_BLOCK, ATT_BLOCK), BF16),
                        pltpu.VMEM((ATT_REP * ATT_BLOCK, ATT_BLOCK), BF16)]
        + [pltpu.VMEM((_ext_rows(d), ATT_HEAD_DIM), BF16) for d in ATT_DILATIONS]
        + [pltpu.VMEM((_ext_rows(d), 2 * ATT_HEAD_DIM), BF16) for d in ATT_DILATIONS],
        compiler_params=_params(("parallel", "parallel", "parallel")),
        name="attn",
    )(*qs, *ks, *vs, *ks, *vs)


def _permute_head_dims(w, n_heads):
    half = ROPE_DIM // 2
    n_low = LANES // 2 - half
    w = w.reshape(w.shape[0], n_heads, ATT_HEAD_DIM)
    w = jnp.concatenate([w[..., :half], w[..., ROPE_DIM:ROPE_DIM + n_low], w[..., half:ROPE_DIM],
                         w[..., ROPE_DIM + n_low:]], axis=-1)
    return w.reshape(w.shape[0], n_heads * ATT_HEAD_DIM)


def _rope_tables(seq):
    half = ROPE_DIM // 2
    inv_freq = jnp.power(jnp.float32(ROPE_THETA), -jnp.arange(0, ROPE_DIM, 2, dtype=F32) / ROPE_DIM)
    ang = jnp.arange(seq, dtype=jnp.int32).astype(F32)[:, None] * inv_freq[None, :]
    cos, sin = jnp.cos(ang), jnp.sin(ang)
    ones = jnp.ones((seq, LANES // 2 - half), F32)
    zeros = jnp.zeros_like(ones)
    cos_t = jnp.concatenate([cos, ones, cos, ones], axis=1)
    sin_t = jnp.concatenate([-sin, zeros, sin, zeros], axis=1)
    regroup = lambda tbl: tbl.reshape(
        seq // ROW_TILE, QKV_M, ATT_MAX_DIL, LANES).transpose(0, 2, 1, 3).reshape(seq, LANES)
    return regroup(cos_t), regroup(sin_t)


def _head_expand():
    head = jnp.arange(LANES)[:, None]
    chan_head = (jnp.arange(SSM_D_INNER) // SSM_HEAD_DIM)[None, :]
    e = (head == chan_head).astype(BF16)
    return jnp.concatenate([e, e], axis=0)


def _dt_weights(w):
    w = jnp.pad(w, ((0, 0), (0, LANES - SSM_N_HEADS)))
    hi = w.astype(BF16)
    lo = (w - hi.astype(F32)).astype(BF16)
    return jnp.concatenate([hi, hi, lo], axis=0)


def kernel(x, a_norm, ssm_w_in, ssm_conv_w, ssm_conv_b, ssm_dt_bias, ssm_a_log, ssm_d, ssm_norm, ssm_w_out,
           kv_norm, w_kv, b_norm, att_w_q, att_w_o, ffn_norm, ffn_w_up, ffn_conv_w, ffn_w_down, final_norm):
    bsz, seq, dm = x.shape
    assert dm == D_MODEL and seq % ATT_TILE == 0
    assert a_norm.shape[0] == 1 and b_norm.shape[0] == 1 and ffn_norm.shape[0] == 2
    t = bsz * seq
    x2 = x.reshape(t, dm)
    row = lambda v: v.reshape(1, -1).astype(F32)

    w_in = ssm_w_in[0]
    wz = w_in[:, :SSM_D_INNER].astype(BF16)
    wx = w_in[:, SSM_D_INNER:SSM_D_INNER + SSM_CONV_DIM].astype(BF16)
    wdt3 = _dt_weights(w_in[:, SSM_D_INNER + SSM_CONV_DIM:])
    pad_heads = lambda v: jnp.pad(row(v), ((0, 0), (0, LANES - SSM_N_HEADS)))
    z, xbc, dt = _in_proj(x2, row(a_norm[0]), wz, wx, wdt3, ssm_conv_w[0], row(ssm_conv_b[0]), bsz, seq)
    x2 = _ssd(xbc, z, dt, x2, pad_heads(ssm_dt_bias[0]), pad_heads(ssm_a_log[0]),
              row(jnp.repeat(ssm_d[0], SSM_HEAD_DIM)), row(ssm_norm[0]), _head_expand(), ssm_w_out[0].astype(BF16),
              bsz, seq)
    x2 = _ffn(x2, None, None, row(ffn_norm[0]), ffn_w_up[0].astype(BF16), ffn_conv_w[0],
              ffn_w_down[0].astype(BF16), row(final_norm), bsz, seq, False)

    cos_t, sin_t = _rope_tables(seq)
    n_q_heads = len(ATT_DILATIONS) * ATT_HEADS
    n_k_heads = len(ATT_DILATIONS) * ATT_KV_HEADS
    k_cols = n_k_heads * ATT_HEAD_DIM
    wq = _permute_head_dims(att_w_q[0], n_q_heads).astype(BF16)
    wkv = jnp.concatenate([_permute_head_dims(w_kv[:, :k_cols], n_k_heads), w_kv[:, k_cols:]], axis=1).astype(BF16)
    outs = _qkv(x2, row(kv_norm), row(b_norm[0]), wq, wkv, cos_t, sin_t, bsz, seq)
    o = _attn(outs[0:3], outs[3:6], outs[6:9], bsz, seq)
    x2 = _ffn(x2, o.reshape(t, dm), att_w_o[0].astype(BF16), row(ffn_norm[1]), ffn_w_up[1].astype(BF16),
              ffn_conv_w[1], ffn_w_down[1].astype(BF16), row(final_norm), bsz, seq, True)
    return x2.reshape(bsz, seq, dm)
```

```python
import functools
import math

import jax
import jax.numpy as jnp
from jax import lax
from jax.experimental import pallas as pl
from jax.experimental.pallas import tpu as pltpu

F32 = jnp.float32
BF16 = jnp.bfloat16

D_MODEL = 1024
RMS_EPS = 1e-6
GATED_NORM_EPS = 1e-5

SSM_D_INNER = 2048
SSM_HEAD_DIM = 64
SSM_N_HEADS = 32
SSM_N_GROUPS = 8
SSM_HEADS_PER_GROUP = SSM_N_HEADS // SSM_N_GROUPS
SSM_GROUP_CH = SSM_D_INNER // SSM_N_GROUPS
SSM_D_STATE = 128
SSM_CONV = 4
SSM_CHUNK = 128
SSM_BC_DIM = SSM_N_GROUPS * SSM_D_STATE
SSM_CONV_DIM = SSM_D_INNER + 2 * SSM_BC_DIM

ATT_DILATIONS = (1, 4, 16)
ATT_MAX_DIL = max(ATT_DILATIONS)
ATT_BLOCK = 128
ATT_HEAD_DIM = 128
ATT_HEADS = 8
ATT_KV_HEADS = 2
ATT_REP = ATT_HEADS // ATT_KV_HEADS
ATT_TILE = ATT_BLOCK * ATT_MAX_DIL
ATT_UNITS = ATT_TILE // ATT_BLOCK
ATT_Q_GROUP_DIM = ATT_HEADS * ATT_HEAD_DIM
ATT_KV_GROUP_DIM = ATT_KV_HEADS * ATT_HEAD_DIM
ROPE_DIM = 32
ROPE_THETA = 500000.0

FFN_DIM = 2816
FFN_CONV = 3
FFN_COL = 256

LANES = 128
SUBLANES = 8
BF16_ROWS = 2 * SUBLANES
NEG_BIG = -1e30
VMEM_LIMIT = 56 * 1024 * 1024

ROW_TILE = 512
ROW_GROUPS = ROW_TILE // SUBLANES


def _const_spec(shape):
    nd = len(shape)
    return pl.BlockSpec(shape, lambda *_: (0,) * nd, pipeline_mode=pl.Buffered(1))


def _params(sem):
    return pltpu.CompilerParams(dimension_semantics=sem, vmem_limit_bytes=VMEM_LIMIT)


def _silu(v):
    return v / (1.0 + jnp.exp2(v * (-math.log2(math.e))))


def _log1p(v):
    u = 1.0 + v
    return jnp.where(u == 1.0, v, jnp.log(u) * (v / (u - 1.0)))


def _rms_normed(x, w):
    return x * lax.rsqrt(jnp.mean(x * x, axis=-1, keepdims=True) + RMS_EPS) * w


def _pipelined(jobs):
    pending = jobs[0][0]()
    for i, (_, epilogue) in enumerate(jobs):
        upcoming = jobs[i + 1][0]() if i + 1 < len(jobs) else None
        epilogue(pending)
        pending = upcoming


def _permute_rows(v, scr_ref):
    n_ct = v.shape[1] // LANES
    for ct in range(n_ct):
        for s in range(SUBLANES):
            scr_ref[ct, pl.ds(s, ROW_GROUPS, stride=SUBLANES), :] = (
                v[s * ROW_GROUPS:(s + 1) * ROW_GROUPS, ct * LANES:(ct + 1) * LANES])
    return jnp.concatenate([scr_ref[ct] for ct in range(n_ct)], axis=1)


def _unpermute_rows(v, scr_ref, emit):
    n_ct = v.shape[1] // LANES
    for ct in range(n_ct):
        scr_ref[ct] = v[:, ct * LANES:(ct + 1) * LANES]
    for ct in range(n_ct):
        for s in range(SUBLANES):
            emit(s, ct, scr_ref[ct, pl.ds(s, ROW_GROUPS, stride=SUBLANES), :])


CONV_STRIP = 32


def _conv_halo(u, prev_tail, n_taps):
    halo = n_taps - 1
    tail = u[ROW_TILE - halo * SUBLANES:, :]
    sub = lax.broadcasted_iota(jnp.int32, (SUBLANES, u.shape[1]), 0)
    fix = []
    for i in range(halo):
        rows = slice(i * SUBLANES, (i + 1) * SUBLANES)
        fix.append(jnp.where(sub == 0, pltpu.roll(prev_tail[rows], 1, axis=0), pltpu.roll(tail[rows], 1, axis=0)))
    return jnp.concatenate(fix, axis=0), tail


def _conv_strip(u, fix, taps, n_taps, r0):
    halo = n_taps - 1
    out = taps[halo] * u[r0:r0 + CONV_STRIP]
    for k in range(1, halo + 1):
        start = r0 - k * SUBLANES
        if start >= 0:
            shifted = u[start:start + CONV_STRIP]
        else:
            shifted = jnp.concatenate([fix[halo * SUBLANES + start:], u[:start + CONV_STRIP]], axis=0)
        out = out + taps[halo - k] * shifted
    return out


IN_NC = 512
SSM_HALO_ROWS = (SSM_CONV - 1) * SUBLANES


def _in_proj_kernel(x_ref, nw_ref, wz_ref, wx_ref, wdt_ref, cw_ref, cb_ref, z_ref, xbc_ref, dt_ref,
                    perm_ref, out_ref, tail_ref):
    @pl.when(pl.program_id(1) == 0)
    def _():
        tail_ref[...] = jnp.zeros_like(tail_ref)

    h = _rms_normed(x_ref[...], nw_ref[...])
    hb = h.astype(BF16)
    hp = _permute_rows(h, perm_ref).astype(BF16)

    n_x = SSM_CONV_DIM // IN_NC
    z_nc = SSM_D_INNER // n_x

    def chunk_job(i):
        cols = slice(i * IN_NC, (i + 1) * IN_NC)
        zcols = slice(i * z_nc, (i + 1) * z_nc)

        def matmul():
            return (jnp.dot(hp, wx_ref[:, cols], preferred_element_type=F32),
                    jnp.dot(hb, wz_ref[:, zcols], preferred_element_type=F32))

        def epilogue(res):
            u, uz = res
            z_ref[:, zcols] = uz.astype(BF16)
            taps = [cw_ref[k:k + 1, cols] for k in range(SSM_CONV)]
            bias = cb_ref[:, cols]
            fix, tail = _conv_halo(u, tail_ref[:, cols], SSM_CONV)
            tail_ref[:, cols] = tail
            stage = out_ref.at[i % 2]
            for r0 in range(0, ROW_TILE, CONV_STRIP):
                act = _silu(_conv_strip(u, fix, taps, SSM_CONV, r0) + bias)
                for ct in range(IN_NC // LANES):
                    stage[ct, r0:r0 + CONV_STRIP, :] = act[:, ct * LANES:(ct + 1) * LANES]
            for ct in range(IN_NC // LANES):
                c0 = i * IN_NC + ct * LANES
                for s in range(SUBLANES):
                    xbc_ref[s * ROW_GROUPS:(s + 1) * ROW_GROUPS, c0:c0 + LANES] = (
                        stage[ct, pl.ds(s, ROW_GROUPS, stride=SUBLANES), :].astype(BF16))

        return matmul, epilogue

    def dt_job():
        def matmul():
            h_lo = (h - hb.astype(F32)).astype(BF16)
            return jnp.dot(jnp.concatenate([hb, h_lo, hb], axis=1), wdt_ref[...], preferred_element_type=F32)

        def epilogue(u):
            dt_ref[...] = u

        return matmul, epilogue

    _pipelined([chunk_job(i) for i in range(n_x)] + [dt_job()])


def _in_proj(x2, nw, wz, wx, wdt3, cw, cb, bsz, seq):
    nt = seq // ROW_TILE
    rows = lambda b, s: (b * nt + s, 0)
    t = bsz * seq
    return pl.pallas_call(
        _in_proj_kernel,
        grid=(bsz, nt),
        in_specs=[
            pl.BlockSpec((ROW_TILE, D_MODEL), rows),
            _const_spec((1, D_MODEL)),
            _const_spec((D_MODEL, SSM_D_INNER)),
            _const_spec((D_MODEL, SSM_CONV_DIM)),
            _const_spec((3 * D_MODEL, LANES)),
            _const_spec((SSM_CONV, SSM_CONV_DIM)),
            _const_spec((1, SSM_CONV_DIM)),
        ],
        out_specs=[
            pl.BlockSpec((ROW_TILE, SSM_D_INNER), rows),
            pl.BlockSpec((ROW_TILE, SSM_CONV_DIM), rows),
            pl.BlockSpec((ROW_TILE, LANES), rows),
        ],
        out_shape=[
            jax.ShapeDtypeStruct((t, SSM_D_INNER), BF16),
            jax.ShapeDtypeStruct((t, SSM_CONV_DIM), BF16),
            jax.ShapeDtypeStruct((t, LANES), F32),
        ],
        scratch_shapes=[
            pltpu.VMEM((D_MODEL // LANES, ROW_TILE, LANES), F32),
            pltpu.VMEM((2, IN_NC // LANES, ROW_TILE, LANES), F32),
            pltpu.VMEM((SSM_HALO_ROWS, SSM_CONV_DIM), F32),
        ],
        compiler_params=_params(("parallel", "arbitrary")),
        name="in_proj",
    )(x2, nw, wz, wx, wdt3, cw, cb)


def _split_hi_lo(v):
    hi = v.astype(BF16)
    lo = (v - hi.astype(F32)).astype(BF16)
    return jnp.concatenate([hi, lo], axis=1)


SSD_CHUNKS_PER_STEP = 4
SSD_EXP_ROWS = 3 * SSM_CHUNK + BF16_ROWS
SSD_EXP_TILE = 256


def _ssd_kernel(xbc_ref, z_ref, dt_ref, dtn_ref, x_ref, dtb_ref, alog_ref, dsk_ref, nw_ref, exp_ref, wout_ref, o_ref,
                y_ref, state_ref, acs_a, acst_a, dexp_a, acs_b, acst_b, dexp_b):
    L = SSM_CHUNK
    P = SSM_HEAD_DIM
    half_chunks = SSD_CHUNKS_PER_STEP // 2
    buf_a = (acs_a, acst_a, dexp_a)
    buf_b = (acs_b, acst_b, dexp_b)

    row = lax.broadcasted_iota(jnp.int32, (L, L), 0)
    col = lax.broadcasted_iota(jnp.int32, (L, L), 1)
    causal = row >= col
    tril = jnp.where(causal, 1.0, 0.0).astype(BF16)
    lane_head = lax.broadcasted_iota(jnp.int32, (L, SSM_GROUP_CH), 1) // P
    head_mask = [jnp.where(lane_head == hh, 1.0, 0.0).astype(BF16) for hh in range(SSM_HEADS_PER_GROUP)]

    def decay_stages(src_ref, rows, buf, ci):
        acs_ref, acst_ref, dexp_ref = buf
        v = {}

        def steps():
            dt_raw = src_ref[rows, :] + dtb_ref[...]
            dt = jnp.maximum(dt_raw, 0.0) + _log1p(jnp.exp(-jnp.abs(dt_raw)))
            adt = dt * (-jnp.exp(alog_ref[...]))
            a_hi = adt.astype(BF16)
            a_r1 = adt - a_hi.astype(F32)
            a_mid = a_r1.astype(BF16)
            v.update(dt=dt, parts=(a_hi, a_mid, (a_r1 - a_mid.astype(F32)).astype(BF16)))

        def cumsum():
            v["a_cs"] = sum(jnp.dot(tril, part, preferred_element_type=F32) for part in v["parts"])

        def scalars():
            a_cs, dt = v["a_cs"], v["dt"]
            a_last = a_cs[L - 1:L, :]
            w_state = dt * jnp.exp(a_last - a_cs)
            e_acs = jnp.exp(a_cs)
            e_last = jnp.broadcast_to(jnp.exp(a_last), (BF16_ROWS, LANES))
            v["lhs"] = _split_hi_lo(jnp.concatenate([dt, w_state, e_acs, e_last], axis=0))
            acs_ref[ci] = a_cs
            acst_ref[ci] = a_cs.T

        def expand(j):
            cols = slice(j * SSD_EXP_TILE, (j + 1) * SSD_EXP_TILE)
            dexp_ref[ci, :, cols] = jnp.dot(v["lhs"], exp_ref[:, cols], preferred_element_type=F32)

        return [steps, cumsum, scalars] + [functools.partial(expand, j) for j in range(SSM_D_INNER // SSD_EXP_TILE)]

    def scan(rows, buf, ci, between_groups):
        acs_ref, acst_ref, dexp_ref = buf
        a_cs = acs_ref[ci]
        a_cs_t = acst_ref[ci]
        half = SSM_N_GROUPS // 2
        proj = None
        for g in range(SSM_N_GROUPS):
            xcols = slice(g * SSM_GROUP_CH, (g + 1) * SSM_GROUP_CH)
            x_g = xbc_ref[rows, xcols].astype(F32)
            b_bf = xbc_ref[rows, SSM_D_INNER + g * SSM_D_STATE:SSM_D_INNER + (g + 1) * SSM_D_STATE]
            c_bf = xbc_ref[rows, SSM_D_INNER + SSM_BC_DIM + g * SSM_D_STATE:
                           SSM_D_INNER + SSM_BC_DIM + (g + 1) * SSM_D_STATE]
            cb = lax.dot_general(c_bf, b_bf, (((1,), (1,)), ((), ())), preferred_element_type=F32)
            prev = state_ref[g]
            y_off = jnp.dot(c_bf, prev.astype(BF16), preferred_element_type=F32)
            xdt = (x_g * dexp_ref[ci, 0:L, xcols]).astype(BF16)
            xw = (x_g * dexp_ref[ci, L:2 * L, xcols]).astype(BF16)

            lhs, rhs = [], []
            for hh in range(SSM_HEADS_PER_GROUP):
                h = g * SSM_HEADS_PER_GROUP + hh
                seg = jnp.broadcast_to(a_cs[:, h:h + 1], (L, L)) - a_cs_t[h:h + 1, :]
                lhs.append((cb * jnp.exp(jnp.where(causal, seg, NEG_BIG))).astype(BF16))
                rhs.append(xdt * head_mask[hh])
            y_g = jnp.dot(jnp.concatenate(lhs, axis=1), jnp.concatenate(rhs, axis=0), preferred_element_type=F32)
            y_g = y_g + y_off * dexp_ref[ci, 2 * L:3 * L, xcols] + x_g * dsk_ref[:, xcols]

            new_state = jnp.dot(b_bf.T, xw, preferred_element_type=F32)
            state_ref[g] = prev * dexp_ref[ci, 3 * L:3 * L + 1, xcols] + new_state

            y_g = y_g * _silu(z_ref[rows, xcols].astype(F32))
            y_g = y_g * lax.rsqrt(jnp.mean(y_g * y_g, axis=-1, keepdims=True) + GATED_NORM_EPS)
            y_ref[rows, xcols] = (y_g * nw_ref[:, xcols]).astype(BF16)

            if g % half == half - 1:
                ks = slice((g + 1 - half) * SSM_GROUP_CH, (g + 1) * SSM_GROUP_CH)
                part = jnp.dot(y_ref[rows, ks], wout_ref[ks, :], preferred_element_type=F32)
                proj = part if proj is None else proj + part
            for stage in between_groups[g]:
                stage()
        o_ref[rows, :] = x_ref[rows, :] + proj

    chunk_rows = [slice(ci * L, (ci + 1) * L) for ci in range(SSD_CHUNKS_PER_STEP)]

    def spread(stages, n_slots):
        out = [[] for _ in range(n_slots)]
        for i, stage in enumerate(stages):
            out[i * n_slots // len(stages)].append(stage)
        return out

    @pl.when(pl.program_id(1) == 0)
    def _():
        state_ref[...] = jnp.zeros_like(state_ref)
        for ci in range(half_chunks):
            for stage in decay_stages(dt_ref, chunk_rows[ci], buf_a, ci):
                stage()

    n_slots = half_chunks * SSM_N_GROUPS
    fill_b = spread([st for ci in range(half_chunks)
                     for st in decay_stages(dt_ref, chunk_rows[half_chunks + ci], buf_b, ci)], n_slots)
    for ci in range(half_chunks):
        scan(chunk_rows[ci], buf_a, ci, fill_b[ci * SSM_N_GROUPS:(ci + 1) * SSM_N_GROUPS])
    fill_a = spread([st for ci in range(half_chunks)
                     for st in decay_stages(dtn_ref, chunk_rows[ci], buf_a, ci)], n_slots)
    for ci in range(half_chunks):
        scan(chunk_rows[half_chunks + ci], buf_b, ci, fill_a[ci * SSM_N_GROUPS:(ci + 1) * SSM_N_GROUPS])


def _ssd(xbc, z, dt, x2, dtb, alog, dsk, nw, expand, wout, bsz, seq):
    L = SSD_CHUNKS_PER_STEP * SSM_CHUNK
    nc = seq // L
    rows = lambda b, c: (b * nc + c, 0)
    decay_buf = [pltpu.VMEM((SSD_CHUNKS_PER_STEP // 2, SSM_CHUNK, LANES), F32),
                 pltpu.VMEM((SSD_CHUNKS_PER_STEP // 2, LANES, SSM_CHUNK), F32),
                 pltpu.VMEM((SSD_CHUNKS_PER_STEP // 2, SSD_EXP_ROWS, SSM_D_INNER), F32)]
    return pl.pallas_call(
        _ssd_kernel,
        grid=(bsz, nc),
        in_specs=[
            pl.BlockSpec((L, SSM_CONV_DIM), rows),
            pl.BlockSpec((L, SSM_D_INNER), rows),
            pl.BlockSpec((L, LANES), rows),
            pl.BlockSpec((L, LANES), lambda b, c: (b * nc + jnp.minimum(c + 1, nc - 1), 0)),
            pl.BlockSpec((L, D_MODEL), rows),
            _const_spec((1, LANES)),
            _const_spec((1, LANES)),
            _const_spec((1, SSM_D_INNER)),
            _const_spec((1, SSM_D_INNER)),
            _const_spec((2 * LANES, SSM_D_INNER)),
            _const_spec((SSM_D_INNER, D_MODEL)),
        ],
        out_specs=pl.BlockSpec((L, D_MODEL), rows),
        out_shape=jax.ShapeDtypeStruct((bsz * seq, D_MODEL), F32),
        scratch_shapes=[pltpu.VMEM((L, SSM_D_INNER), BF16),
                        pltpu.VMEM((SSM_N_GROUPS, SSM_D_STATE, SSM_GROUP_CH), F32)] + decay_buf + decay_buf,
        compiler_params=_params(("parallel", "arbitrary")),
        name="ssd",
    )(xbc, z, dt, dt, x2, dtb, alog, dsk, nw, expand, wout)


FFN_HALO_ROWS = (FFN_CONV - 1) * SUBLANES


def _ffn_kernel(*refs, has_pre, final_norm):
    if has_pre:
        x_ref, pre_ref, wpre_ref, nw_ref, wup_ref, cw_ref, wdn_ref, fnw_ref, o_ref, perm_ref, tail_ref = refs
    else:
        x_ref, nw_ref, wup_ref, cw_ref, wdn_ref, fnw_ref, o_ref, perm_ref, tail_ref = refs

    @pl.when(pl.program_id(1) == 0)
    def _():
        tail_ref[...] = jnp.zeros_like(tail_ref)

    x1 = x_ref[...]
    if has_pre:
        x1 = x1 + jnp.dot(pre_ref[...], wpre_ref[...], preferred_element_type=F32)
    hp = _permute_rows(_rms_normed(x1, nw_ref[...]), perm_ref).astype(BF16)
    acc = [None]

    def chunk_job(c):
        def matmul():
            return [jnp.dot(hp, wup_ref[:, half * FFN_DIM + c * FFN_COL:half * FFN_DIM + (c + 1) * FFN_COL],
                            preferred_element_type=F32) for half in range(2)]

        def epilogue(us):
            taps, fixes = [], []
            for half, u in enumerate(us):
                cols = slice(half * FFN_DIM + c * FFN_COL, half * FFN_DIM + (c + 1) * FFN_COL)
                taps.append([cw_ref[k:k + 1, cols] for k in range(FFN_CONV)])
                fix, tail = _conv_halo(u, tail_ref[:, cols], FFN_CONV)
                tail_ref[:, cols] = tail
                fixes.append(fix)
            strips = []
            for r0 in range(0, ROW_TILE, CONV_STRIP):
                gate, val = [_conv_strip(us[half], fixes[half], taps[half], FFN_CONV, r0) for half in range(2)]
                strips.append((_silu(gate) * val).astype(BF16))
            act = jnp.concatenate(strips, axis=0)
            part = jnp.dot(act, wdn_ref[c * FFN_COL:(c + 1) * FFN_COL, :], preferred_element_type=F32)
            acc[0] = part if acc[0] is None else acc[0] + part

        return matmul, epilogue

    _pipelined([chunk_job(c) for c in range(FFN_DIM // FFN_COL)])

    def emit(s, ct, block):
        o_ref[s * ROW_GROUPS:(s + 1) * ROW_GROUPS, ct * LANES:(ct + 1) * LANES] = block

    _unpermute_rows(acc[0], perm_ref, emit)
    out = x1 + o_ref[...]
    if final_norm:
        out = _rms_normed(out, fnw_ref[...])
    o_ref[...] = out


def _ffn(x2, pre, wpre, nw, wup, cw, wdn, fnw, bsz, seq, final_norm):
    nt = seq // ROW_TILE
    rows = lambda b, s: (b * nt + s, 0)
    has_pre = pre is not None
    pre_specs = [pl.BlockSpec((ROW_TILE, pre.shape[1]), rows), _const_spec(wpre.shape)] if has_pre else []
    pre_args = (pre, wpre) if has_pre else ()
    return pl.pallas_call(
        functools.partial(_ffn_kernel, has_pre=has_pre, final_norm=final_norm),
        grid=(bsz, nt),
        in_specs=[pl.BlockSpec((ROW_TILE, D_MODEL), rows)] + pre_specs + [
            _const_spec((1, D_MODEL)),
            _const_spec((D_MODEL, 2 * FFN_DIM)),
            _const_spec((FFN_CONV, 2 * FFN_DIM)),
            _const_spec((FFN_DIM, D_MODEL)),
            _const_spec((1, D_MODEL)),
        ],
        out_specs=pl.BlockSpec((ROW_TILE, D_MODEL), rows),
        out_shape=jax.ShapeDtypeStruct((bsz * seq, D_MODEL), F32),
        scratch_shapes=[
            pltpu.VMEM((D_MODEL // LANES, ROW_TILE, LANES), F32),
            pltpu.VMEM((FFN_HALO_ROWS, 2 * FFN_DIM), F32),
        ],
        compiler_params=_params(("parallel", "arbitrary")),
        name="ffn_final" if final_norm else "ffn",
    )(x2, *pre_args, nw, wup, cw, wdn, fnw)


QKV_SUB = ATT_TILE // ROW_TILE
QKV_M = ROW_TILE // ATT_MAX_DIL


def _rotary(v, cos, sin):
    return v * cos + pltpu.roll(v, LANES // 2, axis=1) * sin


def _qkv_kernel(x_ref, kvn_ref, qn_ref, wq_ref, wkv_ref, cos_ref, sin_ref,
                q0_ref, q1_ref, q2_ref, k0_ref, k1_ref, k2_ref, v0_ref, v1_ref, v2_ref, scr_ref):
    tm = ROW_TILE
    for ct in range(D_MODEL // LANES):
        scr_ref[ct] = x_ref[:, ct * LANES:(ct + 1) * LANES]
    x = jnp.concatenate(
        [jnp.concatenate([scr_ref[ct, pl.ds(c, QKV_M, stride=ATT_MAX_DIL), :] for c in range(ATT_MAX_DIL)], axis=0)
         for ct in range(D_MODEL // LANES)], axis=1)
    xn = x * lax.rsqrt(jnp.mean(x * x, axis=-1, keepdims=True) + RMS_EPS)
    h_kv = (xn * kvn_ref[...]).astype(BF16)
    h_q = (xn * qn_ref[...]).astype(BF16)
    cos = cos_ref[...]
    sin = sin_ref[...]
    q_scale = ATT_HEAD_DIM ** -0.5 * math.log2(math.e)

    def emit(res, out_ref, d):
        by_c = [res[c * QKV_M:(c + 1) * QKV_M] for c in range(ATT_MAX_DIL)]
        if d == 16:
            for c in range(ATT_MAX_DIL):
                out_ref[c] = by_c[c].astype(BF16)
        elif d == 4:
            for c in range(ATT_MAX_DIL):
                out_ref[c % 4, c // 4] = by_c[c].astype(BF16)
        else:
            for nl in range(tm // ATT_BLOCK):
                ks = slice(nl * SUBLANES, (nl + 1) * SUBLANES)
                for c in range(0, ATT_MAX_DIL, 2):
                    r0 = nl * ATT_BLOCK + c * SUBLANES
                    out_ref[r0:r0 + BF16_ROWS, :] = jnp.concatenate(
                        [by_c[c][ks], by_c[c + 1][ks]], axis=0).astype(BF16)

    def rot(v, n_heads):
        return jnp.concatenate(
            [_rotary(v[:, hd * LANES:(hd + 1) * LANES], cos, sin) for hd in range(n_heads)], axis=1)

    q_refs = (q0_ref, q1_ref, q2_ref)
    k_refs = (k0_ref, k1_ref, k2_ref)
    v_refs = (v0_ref, v1_ref, v2_ref)
    jobs = []
    for g, d in enumerate(ATT_DILATIONS):
        qc = slice(g * ATT_Q_GROUP_DIM, (g + 1) * ATT_Q_GROUP_DIM)
        kc = slice(g * ATT_KV_GROUP_DIM, (g + 1) * ATT_KV_GROUP_DIM)
        v0 = len(ATT_DILATIONS) * ATT_KV_GROUP_DIM + g * ATT_KV_GROUP_DIM
        vc = slice(v0, v0 + ATT_KV_GROUP_DIM)
        jobs.append((lambda qc=qc: jnp.dot(h_q, wq_ref[:, qc], preferred_element_type=F32),
                     lambda u, g=g, d=d: emit(rot(u, ATT_HEADS) * q_scale, q_refs[g], d)))
        jobs.append((lambda kc=kc: jnp.dot(h_kv, wkv_ref[:, kc], preferred_element_type=F32),
                     lambda u, g=g, d=d: emit(rot(u, ATT_KV_HEADS), k_refs[g], d)))
        jobs.append((lambda vc=vc: jnp.dot(h_kv, wkv_ref[:, vc], preferred_element_type=F32),
                     lambda u, g=g, d=d: emit(u, v_refs[g], d)))
    _pipelined(jobs)


def _qkv(x2, kvn, qn, wq, wkv, cos_t, sin_t, bsz, seq):
    tm = ROW_TILE
    nt = seq // tm
    n_att = seq // ATT_TILE

    def out_arrays(cols):
        shapes, specs = [], []
        for d in ATT_DILATIONS:
            if d == 1:
                shapes.append(jax.ShapeDtypeStruct((bsz * seq, cols), BF16))
                specs.append(pl.BlockSpec((tm, cols), lambda b, s: (b * nt + s, 0)))
            elif d == 4:
                shapes.append(jax.ShapeDtypeStruct((bsz, n_att, 4, QKV_SUB, 4, QKV_M, cols), BF16))
                specs.append(pl.BlockSpec((None, None, 4, None, 4, QKV_M, cols),
                                          lambda b, s: (b, s // QKV_SUB, 0, s % QKV_SUB, 0, 0, 0)))
            else:
                shapes.append(jax.ShapeDtypeStruct((bsz, n_att, d, QKV_SUB, QKV_M, cols), BF16))
                specs.append(pl.BlockSpec((None, None, d, None, QKV_M, cols),
                                          lambda b, s: (b, s // QKV_SUB, 0, s % QKV_SUB, 0, 0)))
        return shapes, specs

    q_shapes, q_specs = out_arrays(ATT_Q_GROUP_DIM)
    k_shapes, k_specs = out_arrays(ATT_KV_GROUP_DIM)
    v_shapes, v_specs = out_arrays(ATT_KV_GROUP_DIM)
    outs = pl.pallas_call(
        _qkv_kernel,
        grid=(bsz, nt),
        in_specs=[
            pl.BlockSpec((tm, D_MODEL), lambda b, s: (b * nt + s, 0)),
            _const_spec((1, D_MODEL)),
            _const_spec((1, D_MODEL)),
            _const_spec(wq.shape),
            _const_spec(wkv.shape),
            pl.BlockSpec((tm, LANES), lambda b, s: (s, 0)),
            pl.BlockSpec((tm, LANES), lambda b, s: (s, 0)),
        ],
        out_specs=q_specs + k_specs + v_specs,
        out_shape=q_shapes + k_shapes + v_shapes,
        scratch_shapes=[pltpu.VMEM((ATT_Q_GROUP_DIM // LANES, tm, LANES), F32)],
        compiler_params=_params(("parallel", "parallel")),
        name="qkv",
    )(x2, kvn, qn, wq, wkv, cos_t, sin_t)
    return [o.reshape(bsz, seq, o.shape[-1]) for o in outs]


ATT_PITCH = ATT_BLOCK + SUBLANES
ATT_UNROLL = ATT_UNITS


def _ext_rows(d):
    return (ATT_UNITS // d + 1) * d * ATT_BLOCK


def _unit_order(g, rho):
    d = ATT_DILATIONS[g]
    if d == 1:
        return (rho % SUBLANES) * ATT_MAX_DIL + rho // SUBLANES
    if d == 4:
        return (rho % QKV_M) * 4 + rho // QKV_M
    return rho


def _state_segments(g, u):
    d = ATT_DILATIONS[g]
    if d == 1:
        return [(pl.multiple_of(c * ATT_PITCH + u * SUBLANES, SUBLANES), c * SUBLANES, SUBLANES)
                for c in range(ATT_MAX_DIL)]
    if d == 4:
        r4, n4 = u // 4, u % 4
        return [(pl.multiple_of((4 * a + r4) * ATT_PITCH + n4 * QKV_M, SUBLANES), a * QKV_M, QKV_M)
                for a in range(4)]
    return [(pl.multiple_of(u * ATT_PITCH, SUBLANES), 0, ATT_BLOCK)]


def _attn_kernel(q0_ref, q1_ref, q2_ref,
                 k0c_ref, k1c_ref, k2c_ref, v0c_ref, v1c_ref, v2c_ref,
                 k0p_ref, k1p_ref, k2p_ref, v0p_ref, v1p_ref, v2p_ref,
                 o_ref, acc_ref, m_ref, l_ref, mask_ref, eye_ref,
                 ke0_ref, ke1_ref, ke2_ref, ve0_ref, ve1_ref, ve2_ref):
    blk = ATT_BLOCK
    no_prev_tile = (pl.program_id(1) == 0).astype(jnp.int32)

    key_row = lax.broadcasted_iota(jnp.int32, (2 * blk, blk), 0)
    qry_row = lax.broadcasted_iota(jnp.int32, (2 * blk, blk), 1)
    in_prev = key_row < blk
    for g in range(len(ATT_DILATIONS)):
        i_s = _unit_order(g, key_row % blk)
        i_q = _unit_order(g, qry_row)
        cur_ok = (~in_prev) & (i_s <= i_q)
        mask_ref[2 * g] = jnp.where((in_prev & (i_s >= i_q)) | cur_ok, 0.0, NEG_BIG).astype(BF16)
        mask_ref[2 * g + 1] = jnp.where(cur_ok, 0.0, NEG_BIG).astype(BF16)
    e_row = lax.broadcasted_iota(jnp.int32, (ATT_REP * blk, blk), 0)
    e_col = lax.broadcasted_iota(jnp.int32, (ATT_REP * blk, blk), 1)
    eye_ref[...] = jnp.where(e_row % blk == e_col, 1.0, 0.0).astype(BF16)

    groups = ((q0_ref, k0c_ref, v0c_ref, k0p_ref, v0p_ref, ke0_ref, ve0_ref),
              (q1_ref, k1c_ref, v1c_ref, k1p_ref, v1p_ref, ke1_ref, ve1_ref),
              (q2_ref, k2c_ref, v2c_ref, k2p_ref, v2p_ref, ke2_ref, ve2_ref))

    for g, d in enumerate(ATT_DILATIONS):
        _, kc_ref, vc_ref, kp_ref, vp_ref, ke_ref, ve_ref = groups[g]
        per_res = ATT_UNITS // d
        ve_ref[:, ATT_HEAD_DIM:] = jnp.ones((_ext_rows(d), ATT_HEAD_DIM), BF16)
        for r in range(d):
            dst = r * (per_res + 1) * blk
            src_prev = (r * per_res + per_res - 1) * blk
            src = r * per_res * blk
            ke_ref[dst:dst + blk, :] = kp_ref[src_prev:src_prev + blk, :]
            ve_ref[dst:dst + blk, 0:ATT_HEAD_DIM] = vp_ref[src_prev:src_prev + blk, :]
            ke_ref[dst + blk:dst + (per_res + 1) * blk, :] = kc_ref[src:src + per_res * blk, :]
            ve_ref[dst + blk:dst + (per_res + 1) * blk, 0:ATT_HEAD_DIM] = vc_ref[src:src + per_res * blk, :]

    def unit(u, g, q_ref, ke_ref, ve_ref):
        d = ATT_DILATIONS[g]
        per_res = ATT_UNITS // d
        r = u // per_res
        n = u % per_res
        r0 = pl.multiple_of(u * blk, blk)
        e0 = pl.multiple_of((r * (per_res + 1) + n) * blk, blk)
        q = jnp.concatenate([q_ref[pl.ds(r0, blk), hd * LANES:(hd + 1) * LANES] for hd in range(ATT_REP)], axis=0)
        q_aug = jnp.concatenate([q, eye_ref[...]], axis=1)
        k_aug = jnp.concatenate([ke_ref[pl.ds(e0, 2 * blk), :],
                                 mask_ref[2 * g + jnp.where(n == 0, no_prev_tile, 0)]], axis=1)
        s = lax.dot_general(q_aug, k_aug, (((1,), (1,)), ((), ())), preferred_element_type=F32)
        m_cur = jnp.max(s, axis=-1, keepdims=True)
        p = jnp.exp2(s - m_cur).astype(BF16)
        pv = jnp.dot(p, ve_ref[pl.ds(e0, 2 * blk), :], preferred_element_type=F32)
        acc = pv[:, :ATT_HEAD_DIM]
        l_new = pv[:, ATT_HEAD_DIM:]
        m_new = jnp.broadcast_to(m_cur, (ATT_REP * blk, LANES))
        segs = _state_segments(g, u)

        def load(ref):
            return jnp.concatenate([ref[hd, pl.ds(s0, n_rows), :] for hd in range(ATT_REP)
                                    for (s0, _, n_rows) in segs], axis=0)

        if g > 0:
            m_old, l_old, a_old = load(m_ref), load(l_ref), load(acc_ref)
            m_c = m_new
            m_new = jnp.maximum(m_old, m_c)
            w_old = jnp.exp2(m_old - m_new)
            w_cur = jnp.exp2(m_c - m_new)
            l_new = w_old * l_old + w_cur * l_new
            acc = w_old * a_old + w_cur * acc
        for hd in range(ATT_REP):
            for (s0, v0, n_rows) in segs:
                rows = slice(hd * blk + v0, hd * blk + v0 + n_rows)
                m_ref[hd, pl.ds(s0, n_rows), :] = m_new[rows]
                l_ref[hd, pl.ds(s0, n_rows), :] = l_new[rows]
                acc_ref[hd, pl.ds(s0, n_rows), :] = acc[rows]

    for g in range(len(ATT_DILATIONS)):
        q_ref, _, _, _, _, ke_ref, ve_ref = groups[g]

        def body(u, carry, g=g, q_ref=q_ref, ke_ref=ke_ref, ve_ref=ve_ref):
            unit(u, g, q_ref, ke_ref, ve_ref)
            return carry

        lax.fori_loop(0, ATT_UNITS, body, 0, unroll=ATT_UNROLL)

    def finish(m, carry):
        p0 = pl.multiple_of(m * ATT_MAX_DIL, ATT_MAX_DIL)
        for hd in range(ATT_REP):
            halves = []
            for j in range(ATT_MAX_DIL // SUBLANES):
                idx = pl.ds(j * SUBLANES * ATT_PITCH + m, SUBLANES, stride=ATT_PITCH)
                halves.append(acc_ref[hd, idx, :] / l_ref[hd, idx, :])
            o_ref[pl.ds(p0, ATT_MAX_DIL), hd * LANES:(hd + 1) * LANES] = jnp.concatenate(halves, axis=0).astype(BF16)
        return carry

    lax.fori_loop(0, ATT_BLOCK, finish, 0, unroll=ATT_UNROLL)


def _attn(qs, ks, vs, bsz, seq):
    n_att = seq // ATT_TILE
    cur = lambda b, i, j: (b, i, j)
    prev = lambda b, i, j: (b, jnp.maximum(i - 1, 0), j)
    q_spec = pl.BlockSpec((None, ATT_TILE, ATT_REP * ATT_HEAD_DIM), cur)
    kv_cur = pl.BlockSpec((None, ATT_TILE, ATT_HEAD_DIM), cur)
    kv_prev = pl.BlockSpec((None, ATT_TILE, ATT_HEAD_DIM), prev)
    width = ATT_REP * ATT_HEAD_DIM
    stat = pltpu.VMEM((ATT_REP, ATT_MAX_DIL * ATT_PITCH, ATT_HEAD_DIM), F32)
    return pl.pallas_call(
        _attn_kernel,
        grid=(bsz, n_att, ATT_KV_HEADS),
        in_specs=[q_spec] * 3 + [kv_cur] * 6 + [kv_prev] * 6,
        out_specs=pl.BlockSpec((None, ATT_TILE, width), cur),
        out_shape=jax.ShapeDtypeStruct((bsz, seq, ATT_HEADS * ATT_HEAD_DIM), BF16),
        scratch_shapes=[stat, stat, stat,
                        pltpu.VMEM((2 * len(ATT_DILATIONS), 2 * ATT_BLOCK, ATT_BLOCK), BF16),
                        pltpu.VMEM((ATT_REP * ATT_BLOCK, ATT_BLOCK), BF16)]
        + [pltpu.VMEM((_ext_rows(d), ATT_HEAD_DIM), BF16) for d in ATT_DILATIONS]
        + [pltpu.VMEM((_ext_rows(d), 2 * ATT_HEAD_DIM), BF16) for d in ATT_DILATIONS],
        compiler_params=_params(("parallel", "parallel", "parallel")),
        name="attn",
    )(*qs, *ks, *vs, *ks, *vs)


def _permute_head_dims(w, n_heads):
    half = ROPE_DIM // 2
    n_low = LANES // 2 - half
    w = w.reshape(w.shape[0], n_heads, ATT_HEAD_DIM)
    w = jnp.concatenate([w[..., :half], w[..., ROPE_DIM:ROPE_DIM + n_low], w[..., half:ROPE_DIM],
                         w[..., ROPE_DIM + n_low:]], axis=-1)
    return w.reshape(w.shape[0], n_heads * ATT_HEAD_DIM)


def _rope_tables(seq):
    half = ROPE_DIM // 2
    inv_freq = jnp.power(jnp.float32(ROPE_THETA), -jnp.arange(0, ROPE_DIM, 2, dtype=F32) / ROPE_DIM)
    ang = jnp.arange(seq, dtype=jnp.int32).astype(F32)[:, None] * inv_freq[None, :]
    cos, sin = jnp.cos(ang), jnp.sin(ang)
    ones = jnp.ones((seq, LANES // 2 - half), F32)
    zeros = jnp.zeros_like(ones)
    cos_t = jnp.concatenate([cos, ones, cos, ones], axis=1)
    sin_t = jnp.concatenate([-sin, zeros, sin, zeros], axis=1)
    regroup = lambda tbl: tbl.reshape(
        seq // ROW_TILE, QKV_M, ATT_MAX_DIL, LANES).transpose(0, 2, 1, 3).reshape(seq, LANES)
    return regroup(cos_t), regroup(sin_t)


def _head_expand():
    head = jnp.arange(LANES)[:, None]
    chan_head = (jnp.arange(SSM_D_INNER) // SSM_HEAD_DIM)[None, :]
    e = (head == chan_head).astype(BF16)
    return jnp.concatenate([e, e], axis=0)


def _dt_weights(w):
    w = jnp.pad(w, ((0, 0), (0, LANES - SSM_N_HEADS)))
    hi = w.astype(BF16)
    lo = (w - hi.astype(F32)).astype(BF16)
    return jnp.concatenate([hi, hi, lo], axis=0)


def kernel(x, a_norm, ssm_w_in, ssm_conv_w, ssm_conv_b, ssm_dt_bias, ssm_a_log, ssm_d, ssm_norm, ssm_w_out,
           kv_norm, w_kv, b_norm, att_w_q, att_w_o, ffn_norm, ffn_w_up, ffn_conv_w, ffn_w_down, final_norm):
    bsz, seq, dm = x.shape
    assert dm == D_MODEL and seq % ATT_TILE == 0
    assert a_norm.shape[0] == 1 and b_norm.shape[0] == 1 and ffn_norm.shape[0] == 2
    t = bsz * seq
    x2 = x.reshape(t, dm)
    row = lambda v: v.reshape(1, -1).astype(F32)

    w_in = ssm_w_in[0]
    wz = w_in[:, :SSM_D_INNER].astype(BF16)
    wx = w_in[:, SSM_D_INNER:SSM_D_INNER + SSM_CONV_DIM].astype(BF16)
    wdt3 = _dt_weights(w_in[:, SSM_D_INNER + SSM_CONV_DIM:])
    pad_heads = lambda v: jnp.pad(row(v), ((0, 0), (0, LANES - SSM_N_HEADS)))
    z, xbc, dt = _in_proj(x2, row(a_norm[0]), wz, wx, wdt3, ssm_conv_w[0], row(ssm_conv_b[0]), bsz, seq)
    x2 = _ssd(xbc, z, dt, x2, pad_heads(ssm_dt_bias[0]), pad_heads(ssm_a_log[0]),
              row(jnp.repeat(ssm_d[0], SSM_HEAD_DIM)), row(ssm_norm[0]), _head_expand(), ssm_w_out[0].astype(BF16),
              bsz, seq)
    x2 = _ffn(x2, None, None, row(ffn_norm[0]), ffn_w_up[0].astype(BF16), ffn_conv_w[0],
              ffn_w_down[0].astype(BF16), row(final_norm), bsz, seq, False)

    cos_t, sin_t = _rope_tables(seq)
    n_q_heads = len(ATT_DILATIONS) * ATT_HEADS
    n_k_heads = len(ATT_DILATIONS) * ATT_KV_HEADS
    k_cols = n_k_heads * ATT_HEAD_DIM
    wq = _permute_head_dims(att_w_q[0], n_q_heads).astype(BF16)
    wkv = jnp.concatenate([_permute_head_dims(w_kv[:, :k_cols], n_k_heads), w_kv[:, k_cols:]], axis=1).astype(BF16)
    outs = _qkv(x2, row(kv_norm), row(b_norm[0]), wq, wkv, cos_t, sin_t, bsz, seq)
    o = _attn(outs[0:3], outs[3:6], outs[6:9], bsz, seq)
    x2 = _ffn(x2, o.reshape(t, dm), att_w_o[0].astype(BF16), row(ffn_norm[1]), ffn_w_up[1].astype(BF16),
              ffn_conv_w[1], ffn_w_down[1].astype(BF16), row(final_norm), bsz, seq, True)
    return x2.reshape(bsz, seq, dm)
```

```python
import functools
import math

import jax
import jax.numpy as jnp
from jax import lax
from jax.experimental import pallas as pl
from jax.experimental.pallas import tpu as pltpu

F32 = jnp.float32
BF16 = jnp.bfloat16

D_MODEL = 1024
RMS_EPS = 1e-6
GATED_NORM_EPS = 1e-5

SSM_D_INNER = 2048
SSM_HEAD_DIM = 64
SSM_N_HEADS = 32
SSM_N_GROUPS = 8
SSM_HEADS_PER_GROUP = SSM_N_HEADS // SSM_N_GROUPS
SSM_GROUP_CH = SSM_D_INNER // SSM_N_GROUPS
SSM_D_STATE = 128
SSM_CONV = 4
SSM_CHUNK = 128
SSM_BC_DIM = SSM_N_GROUPS * SSM_D_STATE
SSM_CONV_DIM = SSM_D_INNER + 2 * SSM_BC_DIM

ATT_DILATIONS = (1, 4, 16)
ATT_MAX_DIL = max(ATT_DILATIONS)
ATT_BLOCK = 128
ATT_HEAD_DIM = 128
ATT_HEADS = 8
ATT_KV_HEADS = 2
ATT_REP = ATT_HEADS // ATT_KV_HEADS
ATT_TILE = ATT_BLOCK * ATT_MAX_DIL
ATT_UNITS = ATT_TILE // ATT_BLOCK
ATT_Q_GROUP_DIM = ATT_HEADS * ATT_HEAD_DIM
ATT_KV_GROUP_DIM = ATT_KV_HEADS * ATT_HEAD_DIM
ROPE_DIM = 32
ROPE_THETA = 500000.0

FFN_DIM = 2816
FFN_CONV = 3
FFN_COL = 256

LANES = 128
SUBLANES = 8
BF16_ROWS = 2 * SUBLANES
NEG_BIG = -1e30
VMEM_LIMIT = 56 * 1024 * 1024

ROW_TILE = 512
ROW_GROUPS = ROW_TILE // SUBLANES


def _const_spec(shape):
    nd = len(shape)
    return pl.BlockSpec(shape, lambda *_: (0,) * nd, pipeline_mode=pl.Buffered(1))


def _params(sem):
    return pltpu.CompilerParams(dimension_semantics=sem, vmem_limit_bytes=VMEM_LIMIT)


def _silu(v):
    return v / (1.0 + jnp.exp2(v * (-math.log2(math.e))))


def _log1p(v):
    u = 1.0 + v
    return jnp.where(u == 1.0, v, jnp.log(u) * (v / (u - 1.0)))


def _rms_normed(x, w):
    return x * lax.rsqrt(jnp.mean(x * x, axis=-1, keepdims=True) + RMS_EPS) * w


def _pipelined(jobs):
    pending = jobs[0][0]()
    for i, (_, epilogue) in enumerate(jobs):
        upcoming = jobs[i + 1][0]() if i + 1 < len(jobs) else None
        epilogue(pending)
        pending = upcoming


def _permute_rows(v, scr_ref):
    n_ct = v.shape[1] // LANES
    for ct in range(n_ct):
        for s in range(SUBLANES):
            scr_ref[ct, pl.ds(s, ROW_GROUPS, stride=SUBLANES), :] = (
                v[s * ROW_GROUPS:(s + 1) * ROW_GROUPS, ct * LANES:(ct + 1) * LANES])
    return jnp.concatenate([scr_ref[ct] for ct in range(n_ct)], axis=1)


def _unpermute_rows(v, scr_ref, emit):
    n_ct = v.shape[1] // LANES
    for ct in range(n_ct):
        scr_ref[ct] = v[:, ct * LANES:(ct + 1) * LANES]
    for ct in range(n_ct):
        for s in range(SUBLANES):
            emit(s, ct, scr_ref[ct, pl.ds(s, ROW_GROUPS, stride=SUBLANES), :])


CONV_STRIP = 32


def _conv_halo(u, prev_tail, n_taps):
    halo = n_taps - 1
    tail = u[ROW_TILE - halo * SUBLANES:, :]
    sub = lax.broadcasted_iota(jnp.int32, (SUBLANES, u.shape[1]), 0)
    fix = []
    for i in range(halo):
        rows = slice(i * SUBLANES, (i + 1) * SUBLANES)
        fix.append(jnp.where(sub == 0, pltpu.roll(prev_tail[rows], 1, axis=0), pltpu.roll(tail[rows], 1, axis=0)))
    return jnp.concatenate(fix, axis=0), tail


def _conv_strip(u, fix, taps, n_taps, r0):
    halo = n_taps - 1
    out = taps[halo] * u[r0:r0 + CONV_STRIP]
    for k in range(1, halo + 1):
        start = r0 - k * SUBLANES
        if start >= 0:
            shifted = u[start:start + CONV_STRIP]
        else:
            shifted = jnp.concatenate([fix[halo * SUBLANES + start:], u[:start + CONV_STRIP]], axis=0)
        out = out + taps[halo - k] * shifted
    return out


IN_NC = 512
SSM_HALO_ROWS = (SSM_CONV - 1) * SUBLANES


def _in_proj_kernel(x_ref, nw_ref, wz_ref, wx_ref, wdt_ref, cw_ref, cb_ref, z_ref, xbc_ref, dt_ref,
                    perm_ref, out_ref, tail_ref):
    @pl.when(pl.program_id(1) == 0)
    def _():
        tail_ref[...] = jnp.zeros_like(tail_ref)

    h = _rms_normed(x_ref[...], nw_ref[...])
    hb = h.astype(BF16)
    hp = _permute_rows(h, perm_ref).astype(BF16)

    n_x = SSM_CONV_DIM // IN_NC
    z_nc = SSM_D_INNER // n_x

    def chunk_job(i):
        cols = slice(i * IN_NC, (i + 1) * IN_NC)
        zcols = slice(i * z_nc, (i + 1) * z_nc)

        def matmul():
            return (jnp.dot(hp, wx_ref[:, cols], preferred_element_type=F32),
                    jnp.dot(hb, wz_ref[:, zcols], preferred_element_type=F32))

        def epilogue(res):
            u, uz = res
            z_ref[:, zcols] = uz.astype(BF16)
            taps = [cw_ref[k:k + 1, cols] for k in range(SSM_CONV)]
            bias = cb_ref[:, cols]
            fix, tail = _conv_halo(u, tail_ref[:, cols], SSM_CONV)
            tail_ref[:, cols] = tail
            stage = out_ref.at[i % 2]
            for r0 in range(0, ROW_TILE, CONV_STRIP):
                act = _silu(_conv_strip(u, fix, taps, SSM_CONV, r0) + bias)
                for ct in range(IN_NC // LANES):
                    stage[ct, r0:r0 + CONV_STRIP, :] = act[:, ct * LANES:(ct + 1) * LANES]
            for ct in range(IN_NC // LANES):
                c0 = i * IN_NC + ct * LANES
                for s in range(SUBLANES):
                    xbc_ref[s * ROW_GROUPS:(s + 1) * ROW_GROUPS, c0:c0 + LANES] = (
                        stage[ct, pl.ds(s, ROW_GROUPS, stride=SUBLANES), :].astype(BF16))

        return matmul, epilogue

    def dt_job():
        def matmul():
            h_lo = (h - hb.astype(F32)).astype(BF16)
            return jnp.dot(jnp.concatenate([hb, h_lo, hb], axis=1), wdt_ref[...], preferred_element_type=F32)

        def epilogue(u):
            dt_ref[...] = u

        return matmul, epilogue

    _pipelined([chunk_job(i) for i in range(n_x)] + [dt_job()])


def _in_proj(x2, nw, wz, wx, wdt3, cw, cb, bsz, seq):
    nt = seq // ROW_TILE
    rows = lambda b, s: (b * nt + s, 0)
    t = bsz * seq
    return pl.pallas_call(
        _in_proj_kernel,
        grid=(bsz, nt),
        in_specs=[
            pl.BlockSpec((ROW_TILE, D_MODEL), rows),
            _const_spec((1, D_MODEL)),
            _const_spec((D_MODEL, SSM_D_INNER)),
            _const_spec((D_MODEL, SSM_CONV_DIM)),
            _const_spec((3 * D_MODEL, LANES)),
            _const_spec((SSM_CONV, SSM_CONV_DIM)),
            _const_spec((1, SSM_CONV_DIM)),
        ],
        out_specs=[
            pl.BlockSpec((ROW_TILE, SSM_D_INNER), rows),
            pl.BlockSpec((ROW_TILE, SSM_CONV_DIM), rows),
            pl.BlockSpec((ROW_TILE, LANES), rows),
        ],
        out_shape=[
            jax.ShapeDtypeStruct((t, SSM_D_INNER), BF16),
            jax.ShapeDtypeStruct((t, SSM_CONV_DIM), BF16),
            jax.ShapeDtypeStruct((t, LANES), F32),
        ],
        scratch_shapes=[
            pltpu.VMEM((D_MODEL // LANES, ROW_TILE, LANES), F32),
            pltpu.VMEM((2, IN_NC // LANES, ROW_TILE, LANES), F32),
            pltpu.VMEM((SSM_HALO_ROWS, SSM_CONV_DIM), F32),
        ],
        compiler_params=_params(("parallel", "arbitrary")),
        name="in_proj",
    )(x2, nw, wz, wx, wdt3, cw, cb)


def _split_hi_lo(v):
    hi = v.astype(BF16)
    lo = (v - hi.astype(F32)).astype(BF16)
    return jnp.concatenate([hi, lo], axis=1)


SSD_CHUNKS_PER_STEP = 2


def _ssd_kernel(xbc_ref, z_ref, dt_ref, x_ref, dtb_ref, alog_ref, dsk_ref, nw_ref, exp_ref, wout_ref, o_ref,
                y_ref, state_ref):
    L = SSM_CHUNK
    P = SSM_HEAD_DIM

    @pl.when(pl.program_id(1) == 0)
    def _():
        state_ref[...] = jnp.zeros_like(state_ref)

    row = lax.broadcasted_iota(jnp.int32, (L, L), 0)
    col = lax.broadcasted_iota(jnp.int32, (L, L), 1)
    causal = row >= col
    tril = jnp.where(causal, 1.0, 0.0).astype(BF16)
    lane_head = lax.broadcasted_iota(jnp.int32, (L, SSM_GROUP_CH), 1) // P
    head_mask = [jnp.where(lane_head == hh, 1.0, 0.0).astype(BF16) for hh in range(SSM_HEADS_PER_GROUP)]
    neg_a = -jnp.exp(alog_ref[...])

    def decays(rows):
        dt_raw = dt_ref[rows, :] + dtb_ref[...]
        dt = jnp.maximum(dt_raw, 0.0) + _log1p(jnp.exp(-jnp.abs(dt_raw)))
        adt = dt * neg_a
        a_hi = adt.astype(BF16)
        a_r1 = adt - a_hi.astype(F32)
        a_mid = a_r1.astype(BF16)
        a_lo = (a_r1 - a_mid.astype(F32)).astype(BF16)
        a_cs = (jnp.dot(tril, a_hi, preferred_element_type=F32) + jnp.dot(tril, a_mid, preferred_element_type=F32)
                + jnp.dot(tril, a_lo, preferred_element_type=F32))
        a_last = a_cs[L - 1:L, :]
        w_state = dt * jnp.exp(a_last - a_cs)
        e_acs = jnp.exp(a_cs)
        e_last = jnp.broadcast_to(jnp.exp(a_last), (BF16_ROWS, LANES))
        expanded = jnp.dot(_split_hi_lo(jnp.concatenate([dt, w_state, e_acs, e_last], axis=0)), exp_ref[...],
                           preferred_element_type=F32)
        return a_cs, a_cs.T, expanded

    def scan(rows, a_cs, a_cs_t, expanded):
        for g in range(SSM_N_GROUPS):
            xcols = slice(g * SSM_GROUP_CH, (g + 1) * SSM_GROUP_CH)
            x_g = xbc_ref[rows, xcols].astype(F32)
            b_bf = xbc_ref[rows, SSM_D_INNER + g * SSM_D_STATE:SSM_D_INNER + (g + 1) * SSM_D_STATE]
            c_bf = xbc_ref[rows, SSM_D_INNER + SSM_BC_DIM + g * SSM_D_STATE:
                           SSM_D_INNER + SSM_BC_DIM + (g + 1) * SSM_D_STATE]
            cb = lax.dot_general(c_bf, b_bf, (((1,), (1,)), ((), ())), preferred_element_type=F32)
            prev = state_ref[g]
            y_off = jnp.dot(c_bf, prev.astype(BF16), preferred_element_type=F32)
            xdt = (x_g * expanded[0:L, xcols]).astype(BF16)
            xw = (x_g * expanded[L:2 * L, xcols]).astype(BF16)

            lhs, rhs = [], []
            for hh in range(SSM_HEADS_PER_GROUP):
                h = g * SSM_HEADS_PER_GROUP + hh
                seg = jnp.broadcast_to(a_cs[:, h:h + 1], (L, L)) - a_cs_t[h:h + 1, :]
                lhs.append((cb * jnp.exp(jnp.where(causal, seg, NEG_BIG))).astype(BF16))
                rhs.append(xdt * head_mask[hh])
            y_g = jnp.dot(jnp.concatenate(lhs, axis=1), jnp.concatenate(rhs, axis=0), preferred_element_type=F32)
            y_g = y_g + y_off * expanded[2 * L:3 * L, xcols] + x_g * dsk_ref[:, xcols]

            new_state = jnp.dot(b_bf.T, xw, preferred_element_type=F32)
            state_ref[g] = prev * expanded[3 * L:3 * L + 1, xcols] + new_state

            y_g = y_g * _silu(z_ref[rows, xcols].astype(F32))
            y_g = y_g * lax.rsqrt(jnp.mean(y_g * y_g, axis=-1, keepdims=True) + GATED_NORM_EPS)
            y_ref[rows, xcols] = (y_g * nw_ref[:, xcols]).astype(BF16)

    chunks = [slice(ci * L, (ci + 1) * L) for ci in range(SSD_CHUNKS_PER_STEP)]
    pre = [decays(rows) for rows in chunks]
    for rows, args in zip(chunks, pre):
        scan(rows, *args)
        o_ref[rows, :] = x_ref[rows, :] + jnp.dot(y_ref[rows, :], wout_ref[...], preferred_element_type=F32)


def _ssd(xbc, z, dt, x2, dtb, alog, dsk, nw, expand, wout, bsz, seq):
    L = SSD_CHUNKS_PER_STEP * SSM_CHUNK
    nc = seq // L
    rows = lambda b, c: (b * nc + c, 0)
    return pl.pallas_call(
        _ssd_kernel,
        grid=(bsz, nc),
        in_specs=[
            pl.BlockSpec((L, SSM_CONV_DIM), rows),
            pl.BlockSpec((L, SSM_D_INNER), rows),
            pl.BlockSpec((L, LANES), rows),
            pl.BlockSpec((L, D_MODEL), rows),
            _const_spec((1, LANES)),
            _const_spec((1, LANES)),
            _const_spec((1, SSM_D_INNER)),
            _const_spec((1, SSM_D_INNER)),
            _const_spec((2 * LANES, SSM_D_INNER)),
            _const_spec((SSM_D_INNER, D_MODEL)),
        ],
        out_specs=pl.BlockSpec((L, D_MODEL), rows),
        out_shape=jax.ShapeDtypeStruct((bsz * seq, D_MODEL), F32),
        scratch_shapes=[pltpu.VMEM((L, SSM_D_INNER), BF16),
                        pltpu.VMEM((SSM_N_GROUPS, SSM_D_STATE, SSM_GROUP_CH), F32)],
        compiler_params=_params(("parallel", "arbitrary")),
        name="ssd",
    )(xbc, z, dt, x2, dtb, alog, dsk, nw, expand, wout)


FFN_HALO_ROWS = (FFN_CONV - 1) * SUBLANES


def _ffn_kernel(*refs, has_pre, final_norm):
    if has_pre:
        x_ref, pre_ref, wpre_ref, nw_ref, wup_ref, cw_ref, wdn_ref, fnw_ref, o_ref, perm_ref, tail_ref, act_ref = refs
    else:
        x_ref, nw_ref, wup_ref, cw_ref, wdn_ref, fnw_ref, o_ref, perm_ref, tail_ref, act_ref = refs

    @pl.when(pl.program_id(1) == 0)
    def _():
        tail_ref[...] = jnp.zeros_like(tail_ref)

    x1 = x_ref[...]
    if has_pre:
        x1 = x1 + jnp.dot(pre_ref[...], wpre_ref[...], preferred_element_type=F32)
    hp = _permute_rows(_rms_normed(x1, nw_ref[...]), perm_ref).astype(BF16)

    def chunk_job(c):
        def matmul():
            return [jnp.dot(hp, wup_ref[:, half * FFN_DIM + c * FFN_COL:half * FFN_DIM + (c + 1) * FFN_COL],
                            preferred_element_type=F32) for half in range(2)]

        def epilogue(us):
            taps, fixes = [], []
            for half, u in enumerate(us):
                cols = slice(half * FFN_DIM + c * FFN_COL, half * FFN_DIM + (c + 1) * FFN_COL)
                taps.append([cw_ref[k:k + 1, cols] for k in range(FFN_CONV)])
                fix, tail = _conv_halo(u, tail_ref[:, cols], FFN_CONV)
                tail_ref[:, cols] = tail
                fixes.append(fix)
            for r0 in range(0, ROW_TILE, CONV_STRIP):
                gate, val = [_conv_strip(us[half], fixes[half], taps[half], FFN_CONV, r0) for half in range(2)]
                act_ref[r0:r0 + CONV_STRIP, c * FFN_COL:(c + 1) * FFN_COL] = (_silu(gate) * val).astype(BF16)

        return matmul, epilogue

    _pipelined([chunk_job(c) for c in range(FFN_DIM // FFN_COL)])
    acc = jnp.dot(act_ref[...], wdn_ref[...], preferred_element_type=F32)

    def emit(s, ct, block):
        o_ref[s * ROW_GROUPS:(s + 1) * ROW_GROUPS, ct * LANES:(ct + 1) * LANES] = block

    _unpermute_rows(acc, perm_ref, emit)
    out = x1 + o_ref[...]
    if final_norm:
        out = _rms_normed(out, fnw_ref[...])
    o_ref[...] = out


def _ffn(x2, pre, wpre, nw, wup, cw, wdn, fnw, bsz, seq, final_norm):
    nt = seq // ROW_TILE
    rows = lambda b, s: (b * nt + s, 0)
    has_pre = pre is not None
    pre_specs = [pl.BlockSpec((ROW_TILE, pre.shape[1]), rows), _const_spec(wpre.shape)] if has_pre else []
    pre_args = (pre, wpre) if has_pre else ()
    return pl.pallas_call(
        functools.partial(_ffn_kernel, has_pre=has_pre, final_norm=final_norm),
        grid=(bsz, nt),
        in_specs=[pl.BlockSpec((ROW_TILE, D_MODEL), rows)] + pre_specs + [
            _const_spec((1, D_MODEL)),
            _const_spec((D_MODEL, 2 * FFN_DIM)),
            _const_spec((FFN_CONV, 2 * FFN_DIM)),
            _const_spec((FFN_DIM, D_MODEL)),
            _const_spec((1, D_MODEL)),
        ],
        out_specs=pl.BlockSpec((ROW_TILE, D_MODEL), rows),
        out_shape=jax.ShapeDtypeStruct((bsz * seq, D_MODEL), F32),
        scratch_shapes=[
            pltpu.VMEM((D_MODEL // LANES, ROW_TILE, LANES), F32),
            pltpu.VMEM((FFN_HALO_ROWS, 2 * FFN_DIM), F32),
            pltpu.VMEM((ROW_TILE, FFN_DIM), BF16),
        ],
        compiler_params=_params(("parallel", "arbitrary")),
        name="ffn_final" if final_norm else "ffn",
    )(x2, *pre_args, nw, wup, cw, wdn, fnw)


QKV_SUB = ATT_TILE // ROW_TILE
QKV_M = ROW_TILE // ATT_MAX_DIL


def _rotary(v, cos, sin):
    return v * cos + pltpu.roll(v, LANES // 2, axis=1) * sin


def _qkv_kernel(x_ref, kvn_ref, qn_ref, wq_ref, wkv_ref, cos_ref, sin_ref,
                q0_ref, q1_ref, q2_ref, k0_ref, k1_ref, k2_ref, v0_ref, v1_ref, v2_ref, scr_ref):
    tm = ROW_TILE
    for ct in range(D_MODEL // LANES):
        scr_ref[ct] = x_ref[:, ct * LANES:(ct + 1) * LANES]
    x = jnp.concatenate(
        [jnp.concatenate([scr_ref[ct, pl.ds(c, QKV_M, stride=ATT_MAX_DIL), :] for c in range(ATT_MAX_DIL)], axis=0)
         for ct in range(D_MODEL // LANES)], axis=1)
    xn = x * lax.rsqrt(jnp.mean(x * x, axis=-1, keepdims=True) + RMS_EPS)
    h_kv = (xn * kvn_ref[...]).astype(BF16)
    h_q = (xn * qn_ref[...]).astype(BF16)
    cos = cos_ref[...]
    sin = sin_ref[...]
    q_scale = ATT_HEAD_DIM ** -0.5 * math.log2(math.e)

    def emit(res, out_ref, d):
        by_c = [res[c * QKV_M:(c + 1) * QKV_M] for c in range(ATT_MAX_DIL)]
        if d == 16:
            for c in range(ATT_MAX_DIL):
                out_ref[c] = by_c[c].astype(BF16)
        elif d == 4:
            for c in range(ATT_MAX_DIL):
                out_ref[c % 4, c // 4] = by_c[c].astype(BF16)
        else:
            for nl in range(tm // ATT_BLOCK):
                ks = slice(nl * SUBLANES, (nl + 1) * SUBLANES)
                for c in range(0, ATT_MAX_DIL, 2):
                    r0 = nl * ATT_BLOCK + c * SUBLANES
                    out_ref[r0:r0 + BF16_ROWS, :] = jnp.concatenate(
                        [by_c[c][ks], by_c[c + 1][ks]], axis=0).astype(BF16)

    def rot(v, n_heads):
        return jnp.concatenate(
            [_rotary(v[:, hd * LANES:(hd + 1) * LANES], cos, sin) for hd in range(n_heads)], axis=1)

    q_refs = (q0_ref, q1_ref, q2_ref)
    k_refs = (k0_ref, k1_ref, k2_ref)
    v_refs = (v0_ref, v1_ref, v2_ref)
    jobs = []
    for g, d in enumerate(ATT_DILATIONS):
        qc = slice(g * ATT_Q_GROUP_DIM, (g + 1) * ATT_Q_GROUP_DIM)
        kc = slice(g * ATT_KV_GROUP_DIM, (g + 1) * ATT_KV_GROUP_DIM)
        v0 = len(ATT_DILATIONS) * ATT_KV_GROUP_DIM + g * ATT_KV_GROUP_DIM
        vc = slice(v0, v0 + ATT_KV_GROUP_DIM)
        jobs.append((lambda qc=qc: jnp.dot(h_q, wq_ref[:, qc], preferred_element_type=F32),
                     lambda u, g=g, d=d: emit(rot(u, ATT_HEADS) * q_scale, q_refs[g], d)))
        jobs.append((lambda kc=kc: jnp.dot(h_kv, wkv_ref[:, kc], preferred_element_type=F32),
                     lambda u, g=g, d=d: emit(rot(u, ATT_KV_HEADS), k_refs[g], d)))
        jobs.append((lambda vc=vc: jnp.dot(h_kv, wkv_ref[:, vc], preferred_element_type=F32),
                     lambda u, g=g, d=d: emit(u, v_refs[g], d)))
    _pipelined(jobs)


def _qkv(x2, kvn, qn, wq, wkv, cos_t, sin_t, bsz, seq):
    tm = ROW_TILE
    nt = seq // tm
    n_att = seq // ATT_TILE

    def out_arrays(cols):
        shapes, specs = [], []
        for d in ATT_DILATIONS:
            if d == 1:
                shapes.append(jax.ShapeDtypeStruct((bsz * seq, cols), BF16))
                specs.append(pl.BlockSpec((tm, cols), lambda b, s: (b * nt + s, 0)))
            elif d == 4:
                shapes.append(jax.ShapeDtypeStruct((bsz, n_att, 4, QKV_SUB, 4, QKV_M, cols), BF16))
                specs.append(pl.BlockSpec((None, None, 4, None, 4, QKV_M, cols),
                                          lambda b, s: (b, s // QKV_SUB, 0, s % QKV_SUB, 0, 0, 0)))
            else:
                shapes.append(jax.ShapeDtypeStruct((bsz, n_att, d, QKV_SUB, QKV_M, cols), BF16))
                specs.append(pl.BlockSpec((None, None, d, None, QKV_M, cols),
                                          lambda b, s: (b, s // QKV_SUB, 0, s % QKV_SUB, 0, 0)))
        return shapes, specs

    q_shapes, q_specs = out_arrays(ATT_Q_GROUP_DIM)
    k_shapes, k_specs = out_arrays(ATT_KV_GROUP_DIM)
    v_shapes, v_specs = out_arrays(ATT_KV_GROUP_DIM)
    outs = pl.pallas_call(
        _qkv_kernel,
        grid=(bsz, nt),
        in_specs=[
            pl.BlockSpec((tm, D_MODEL), lambda b, s: (b * nt + s, 0)),
            _const_spec((1, D_MODEL)),
            _const_spec((1, D_MODEL)),
            _const_spec(wq.shape),
            _const_spec(wkv.shape),
            pl.BlockSpec((tm, LANES), lambda b, s: (s, 0)),
            pl.BlockSpec((tm, LANES), lambda b, s: (s, 0)),
        ],
        out_specs=q_specs + k_specs + v_specs,
        out_shape=q_shapes + k_shapes + v_shapes,
        scratch_shapes=[pltpu.VMEM((ATT_Q_GROUP_DIM // LANES, tm, LANES), F32)],
        compiler_params=_params(("parallel", "parallel")),
        name="qkv",
    )(x2, kvn, qn, wq, wkv, cos_t, sin_t)
    return [o.reshape(bsz, seq, o.shape[-1]) for o in outs]


ATT_PITCH = ATT_BLOCK + SUBLANES
ATT_UNROLL = ATT_UNITS


def _ext_rows(d):
    return (ATT_UNITS // d + 1) * d * ATT_BLOCK


def _unit_order(g, rho):
    d = ATT_DILATIONS[g]
    if d == 1:
        return (rho % SUBLANES) * ATT_MAX_DIL + rho // SUBLANES
    if d == 4:
        return (rho % QKV_M) * 4 + rho // QKV_M
    return rho


def _state_segments(g, u):
    d = ATT_DILATIONS[g]
    if d == 1:
        return [(pl.multiple_of(c * ATT_PITCH + u * SUBLANES, SUBLANES), c * SUBLANES, SUBLANES)
                for c in range(ATT_MAX_DIL)]
    if d == 4:
        r4, n4 = u // 4, u % 4
        return [(pl.multiple_of((4 * a + r4) * ATT_PITCH + n4 * QKV_M, SUBLANES), a * QKV_M, QKV_M)
                for a in range(4)]
    return [(pl.multiple_of(u * ATT_PITCH, SUBLANES), 0, ATT_BLOCK)]


def _attn_kernel(q0_ref, q1_ref, q2_ref,
                 k0c_ref, k1c_ref, k2c_ref, v0c_ref, v1c_ref, v2c_ref,
                 k0p_ref, k1p_ref, k2p_ref, v0p_ref, v1p_ref, v2p_ref,
                 o_ref, acc_ref, m_ref, l_ref, mask_ref, eye_ref,
                 ke0_ref, ke1_ref, ke2_ref, ve0_ref, ve1_ref, ve2_ref):
    blk = ATT_BLOCK
    no_prev_tile = (pl.program_id(1) == 0).astype(jnp.int32)

    key_row = lax.broadcasted_iota(jnp.int32, (2 * blk, blk), 0)
    qry_row = lax.broadcasted_iota(jnp.int32, (2 * blk, blk), 1)
    in_prev = key_row < blk
    for g in range(len(ATT_DILATIONS)):
        i_s = _unit_order(g, key_row % blk)
        i_q = _unit_order(g, qry_row)
        cur_ok = (~in_prev) & (i_s <= i_q)
        mask_ref[2 * g] = jnp.where((in_prev & (i_s >= i_q)) | cur_ok, 0.0, NEG_BIG).astype(BF16)
        mask_ref[2 * g + 1] = jnp.where(cur_ok, 0.0, NEG_BIG).astype(BF16)
    e_row = lax.broadcasted_iota(jnp.int32, (ATT_REP * blk, blk), 0)
    e_col = lax.broadcasted_iota(jnp.int32, (ATT_REP * blk, blk), 1)
    eye_ref[...] = jnp.where(e_row % blk == e_col, 1.0, 0.0).astype(BF16)

    groups = ((q0_ref, k0c_ref, v0c_ref, k0p_ref, v0p_ref, ke0_ref, ve0_ref),
              (q1_ref, k1c_ref, v1c_ref, k1p_ref, v1p_ref, ke1_ref, ve1_ref),
              (q2_ref, k2c_ref, v2c_ref, k2p_ref, v2p_ref, ke2_ref, ve2_ref))

    for g, d in enumerate(ATT_DILATIONS):
        _, kc_ref, vc_ref, kp_ref, vp_ref, ke_ref, ve_ref = groups[g]
        per_res = ATT_UNITS // d
        ve_ref[:, ATT_HEAD_DIM:] = jnp.ones((_ext_rows(d), ATT_HEAD_DIM), BF16)
        for r in range(d):
            dst = r * (per_res + 1) * blk
            src_prev = (r * per_res + per_res - 1) * blk
            src = r * per_res * blk
            ke_ref[dst:dst + blk, :] = kp_ref[src_prev:src_prev + blk, :]
            ve_ref[dst:dst + blk, 0:ATT_HEAD_DIM] = vp_ref[src_prev:src_prev + blk, :]
            ke_ref[dst + blk:dst + (per_res + 1) * blk, :] = kc_ref[src:src + per_res * blk, :]
            ve_ref[dst + blk:dst + (per_res + 1) * blk, 0:ATT_HEAD_DIM] = vc_ref[src:src + per_res * blk, :]

    def unit(u, g, q_ref, ke_ref, ve_ref):
        d = ATT_DILATIONS[g]
        per_res = ATT_UNITS // d
        r = u // per_res
        n = u % per_res
        r0 = pl.multiple_of(u * blk, blk)
        e0 = pl.multiple_of((r * (per_res + 1) + n) * blk, blk)
        q = jnp.concatenate([q_ref[pl.ds(r0, blk), hd * LANES:(hd + 1) * LANES] for hd in range(ATT_REP)], axis=0)
        q_aug = jnp.concatenate([q, eye_ref[...]], axis=1)
        k_aug = jnp.concatenate([ke_ref[pl.ds(e0, 2 * blk), :],
                                 mask_ref[2 * g + jnp.where(n == 0, no_prev_tile, 0)]], axis=1)
        s = lax.dot_general(q_aug, k_aug, (((1,), (1,)), ((), ())), preferred_element_type=F32)
        m_cur = jnp.max(s, axis=-1, keepdims=True)
        p = jnp.exp2(s - m_cur).astype(BF16)
        pv = jnp.dot(p, ve_ref[pl.ds(e0, 2 * blk), :], preferred_element_type=F32)
        acc = pv[:, :ATT_HEAD_DIM]
        l_new = pv[:, ATT_HEAD_DIM:]
        m_new = jnp.broadcast_to(m_cur, (ATT_REP * blk, LANES))
        segs = _state_segments(g, u)

        def load(ref):
            return jnp.concatenate([ref[hd, pl.ds(s0, n_rows), :] for hd in range(ATT_REP)
                                    for (s0, _, n_rows) in segs], axis=0)

        if g > 0:
            m_old, l_old, a_old = load(m_ref), load(l_ref), load(acc_ref)
            m_c = m_new
            m_new = jnp.maximum(m_old, m_c)
            w_old = jnp.exp2(m_old - m_new)
            w_cur = jnp.exp2(m_c - m_new)
            l_new = w_old * l_old + w_cur * l_new
            acc = w_old * a_old + w_cur * acc
        for hd in range(ATT_REP):
            for (s0, v0, n_rows) in segs:
                rows = slice(hd * blk + v0, hd * blk + v0 + n_rows)
                m_ref[hd, pl.ds(s0, n_rows), :] = m_new[rows]
                l_ref[hd, pl.ds(s0, n_rows), :] = l_new[rows]
                acc_ref[hd, pl.ds(s0, n_rows), :] = acc[rows]

    for g in range(len(ATT_DILATIONS)):
        q_ref, _, _, _, _, ke_ref, ve_ref = groups[g]

        def body(u, carry, g=g, q_ref=q_ref, ke_ref=ke_ref, ve_ref=ve_ref):
            unit(u, g, q_ref, ke_ref, ve_ref)
            return carry

        lax.fori_loop(0, ATT_UNITS, body, 0, unroll=ATT_UNROLL)

    def finish(m, carry):
        p0 = pl.multiple_of(m * ATT_MAX_DIL, ATT_MAX_DIL)
        for hd in range(ATT_REP):
            halves = []
            for j in range(ATT_MAX_DIL // SUBLANES):
                idx = pl.ds(j * SUBLANES * ATT_PITCH + m, SUBLANES, stride=ATT_PITCH)
                halves.append(acc_ref[hd, idx, :] / l_ref[hd, idx, :])
            o_ref[pl.ds(p0, ATT_MAX_DIL), hd * LANES:(hd + 1) * LANES] = jnp.concatenate(halves, axis=0).astype(BF16)
        return carry

    lax.fori_loop(0, ATT_BLOCK, finish, 0, unroll=ATT_UNROLL)


def _attn(qs, ks, vs, bsz, seq):
    n_att = seq // ATT_TILE
    cur = lambda b, i, j: (b, i, j)
    prev = lambda b, i, j: (b, jnp.maximum(i - 1, 0), j)
    q_spec = pl.BlockSpec((None, ATT_TILE, ATT_REP * ATT_HEAD_DIM), cur)
    kv_cur = pl.BlockSpec((None, ATT_TILE, ATT_HEAD_DIM), cur)
    kv_prev = pl.BlockSpec((None, ATT_TILE, ATT_HEAD_DIM), prev)
    width = ATT_REP * ATT_HEAD_DIM
    stat = pltpu.VMEM((ATT_REP, ATT_MAX_DIL * ATT_PITCH, ATT_HEAD_DIM), F32)
    return pl.pallas_call(
        _attn_kernel,
        grid=(bsz, n_att, ATT_KV_HEADS),
        in_specs=[q_spec] * 3 + [kv_cur] * 6 + [kv_prev] * 6,
        out_specs=pl.BlockSpec((None, ATT_TILE, width), cur),
        out_shape=jax.ShapeDtypeStruct((bsz, seq, ATT_HEADS * ATT_HEAD_DIM), BF16),
        scratch_shapes=[stat, stat, stat,
                        pltpu.VMEM((2 * len(ATT_DILATIONS), 2 * ATT_BLOCK, ATT_BLOCK), BF16),
                        pltpu.VMEM((ATT_REP * ATT_BLOCK, ATT_BLOCK), BF16)]
        + [pltpu.VMEM((_ext_rows(d), ATT_HEAD_DIM), BF16) for d in ATT_DILATIONS]
        + [pltpu.VMEM((_ext_rows(d), 2 * ATT_HEAD_DIM), BF16) for d in ATT_DILATIONS],
        compiler_params=_params(("parallel", "parallel", "parallel")),
        name="attn",
    )(*qs, *ks, *vs, *ks, *vs)


def _permute_head_dims(w, n_heads):
    half = ROPE_DIM // 2
    n_low = LANES // 2 - half
    w = w.reshape(w.shape[0], n_heads, ATT_HEAD_DIM)
    w = jnp.concatenate([w[..., :half], w[..., ROPE_DIM:ROPE_DIM + n_low], w[..., half:ROPE_DIM],
                         w[..., ROPE_DIM + n_low:]], axis=-1)
    return w.reshape(w.shape[0], n_heads * ATT_HEAD_DIM)


def _rope_tables(seq):
    half = ROPE_DIM // 2
    inv_freq = jnp.power(jnp.float32(ROPE_THETA), -jnp.arange(0, ROPE_DIM, 2, dtype=F32) / ROPE_DIM)
    ang = jnp.arange(seq, dtype=jnp.int32).astype(F32)[:, None] * inv_freq[None, :]
    cos, sin = jnp.cos(ang), jnp.sin(ang)
    ones = jnp.ones((seq, LANES // 2 - half), F32)
    zeros = jnp.zeros_like(ones)
    cos_t = jnp.concatenate([cos, ones, cos, ones], axis=1)
    sin_t = jnp.concatenate([-sin, zeros, sin, zeros], axis=1)
    regroup = lambda tbl: tbl.reshape(
        seq // ROW_TILE, QKV_M, ATT_MAX_DIL, LANES).transpose(0, 2, 1, 3).reshape(seq, LANES)
    return regroup(cos_t), regroup(sin_t)


def _head_expand():
    head = jnp.arange(LANES)[:, None]
    chan_head = (jnp.arange(SSM_D_INNER) // SSM_HEAD_DIM)[None, :]
    e = (head == chan_head).astype(BF16)
    return jnp.concatenate([e, e], axis=0)


def _dt_weights(w):
    w = jnp.pad(w, ((0, 0), (0, LANES - SSM_N_HEADS)))
    hi = w.astype(BF16)
    lo = (w - hi.astype(F32)).astype(BF16)
    return jnp.concatenate([hi, hi, lo], axis=0)


def kernel(x, a_norm, ssm_w_in, ssm_conv_w, ssm_conv_b, ssm_dt_bias, ssm_a_log, ssm_d, ssm_norm, ssm_w_out,
           kv_norm, w_kv, b_norm, att_w_q, att_w_o, ffn_norm, ffn_w_up, ffn_conv_w, ffn_w_down, final_norm):
    bsz, seq, dm = x.shape
    assert dm == D_MODEL and seq % ATT_TILE == 0
    assert a_norm.shape[0] == 1 and b_norm.shape[0] == 1 and ffn_norm.shape[0] == 2
    t = bsz * seq
    x2 = x.reshape(t, dm)
    row = lambda v: v.reshape(1, -1).astype(F32)

    w_in = ssm_w_in[0]
    wz = w_in[:, :SSM_D_INNER].astype(BF16)
    wx = w_in[:, SSM_D_INNER:SSM_D_INNER + SSM_CONV_DIM].astype(BF16)
    wdt3 = _dt_weights(w_in[:, SSM_D_INNER + SSM_CONV_DIM:])
    pad_heads = lambda v: jnp.pad(row(v), ((0, 0), (0, LANES - SSM_N_HEADS)))
    z, xbc, dt = _in_proj(x2, row(a_norm[0]), wz, wx, wdt3, ssm_conv_w[0], row(ssm_conv_b[0]), bsz, seq)
    x2 = _ssd(xbc, z, dt, x2, pad_heads(ssm_dt_bias[0]), pad_heads(ssm_a_log[0]),
              row(jnp.repeat(ssm_d[0], SSM_HEAD_DIM)), row(ssm_norm[0]), _head_expand(), ssm_w_out[0].astype(BF16),
              bsz, seq)
    x2 = _ffn(x2, None, None, row(ffn_norm[0]), ffn_w_up[0].astype(BF16), ffn_conv_w[0],
              ffn_w_down[0].astype(BF16), row(final_norm), bsz, seq, False)

    cos_t, sin_t = _rope_tables(seq)
    n_q_heads = len(ATT_DILATIONS) * ATT_HEADS
    n_k_heads = len(ATT_DILATIONS) * ATT_KV_HEADS
    k_cols = n_k_heads * ATT_HEAD_DIM
    wq = _permute_head_dims(att_w_q[0], n_q_heads).astype(BF16)
    wkv = jnp.concatenate([_permute_head_dims(w_kv[:, :k_cols], n_k_heads), w_kv[:, k_cols:]], axis=1).astype(BF16)
    outs = _qkv(x2, row(kv_norm), row(b_norm[0]), wq, wkv, cos_t, sin_t, bsz, seq)
    o = _attn(outs[0:3], outs[3:6], outs[6:9], bsz, seq)
    x2 = _ffn(x2, o.reshape(t, dm), att_w_o[0].astype(BF16), row(ffn_norm[1]), ffn_w_up[1].astype(BF16),
              ffn_conv_w[1], ffn_w_down[1].astype(BF16), row(final_norm), bsz, seq, True)
    return x2.reshape(bsz, seq, dm)
```

```python
import functools
import math

import jax
import jax.numpy as jnp
from jax import lax
from jax.experimental import pallas as pl
from jax.experimental.pallas import tpu as pltpu

F32 = jnp.float32
BF16 = jnp.bfloat16

D_MODEL = 1024
RMS_EPS = 1e-6
GATED_NORM_EPS = 1e-5

SSM_D_INNER = 2048
SSM_HEAD_DIM = 64
SSM_N_HEADS = 32
SSM_N_GROUPS = 8
SSM_HEADS_PER_GROUP = SSM_N_HEADS // SSM_N_GROUPS
SSM_GROUP_CH = SSM_D_INNER // SSM_N_GROUPS
SSM_D_STATE = 128
SSM_CONV = 4
SSM_CHUNK = 128
SSM_BC_DIM = SSM_N_GROUPS * SSM_D_STATE
SSM_CONV_DIM = SSM_D_INNER + 2 * SSM_BC_DIM

ATT_DILATIONS = (1, 4, 16)
ATT_MAX_DIL = max(ATT_DILATIONS)
ATT_BLOCK = 128
ATT_HEAD_DIM = 128
ATT_HEADS = 8
ATT_KV_HEADS = 2
ATT_REP = ATT_HEADS // ATT_KV_HEADS
ATT_TILE = ATT_BLOCK * ATT_MAX_DIL
ATT_UNITS = ATT_TILE // ATT_BLOCK
ATT_Q_GROUP_DIM = ATT_HEADS * ATT_HEAD_DIM
ATT_KV_GROUP_DIM = ATT_KV_HEADS * ATT_HEAD_DIM
ROPE_DIM = 32
ROPE_THETA = 500000.0

FFN_DIM = 2816
FFN_CONV = 3
FFN_COL = 256

LANES = 128
SUBLANES = 8
BF16_ROWS = 2 * SUBLANES
NEG_BIG = -1e30
VMEM_LIMIT = 56 * 1024 * 1024

ROW_TILE = 512
ROW_GROUPS = ROW_TILE // SUBLANES


def _const_spec(shape):
    nd = len(shape)
    return pl.BlockSpec(shape, lambda *_: (0,) * nd, pipeline_mode=pl.Buffered(1))


def _params(sem):
    return pltpu.CompilerParams(dimension_semantics=sem, vmem_limit_bytes=VMEM_LIMIT)


def _silu(v):
    return v / (1.0 + jnp.exp2(v * (-math.log2(math.e))))


def _log1p(v):
    u = 1.0 + v
    return jnp.where(u == 1.0, v, jnp.log(u) * (v / (u - 1.0)))


def _rms_normed(x, w):
    return x * lax.rsqrt(jnp.mean(x * x, axis=-1, keepdims=True) + RMS_EPS) * w


def _pipelined(jobs):
    pending = jobs[0][0]()
    for i, (_, epilogue) in enumerate(jobs):
        upcoming = jobs[i + 1][0]() if i + 1 < len(jobs) else None
        epilogue(pending)
        pending = upcoming


def _permute_rows(v, scr_ref):
    n_ct = v.shape[1] // LANES
    for ct in range(n_ct):
        for s in range(SUBLANES):
            scr_ref[ct, pl.ds(s, ROW_GROUPS, stride=SUBLANES), :] = (
                v[s * ROW_GROUPS:(s + 1) * ROW_GROUPS, ct * LANES:(ct + 1) * LANES])
    return jnp.concatenate([scr_ref[ct] for ct in range(n_ct)], axis=1)


def _regroup_rows(v, outer, inner):
    cols = v.shape[1]
    return v.reshape(outer, inner, cols).transpose(1, 0, 2).reshape(outer * inner, cols)


def _unpermute_rows(v, scr_ref, emit):
    n_ct = v.shape[1] // LANES
    for ct in range(n_ct):
        scr_ref[ct] = v[:, ct * LANES:(ct + 1) * LANES]
    for ct in range(n_ct):
        for s in range(SUBLANES):
            emit(s, ct, scr_ref[ct, pl.ds(s, ROW_GROUPS, stride=SUBLANES), :])


CONV_STRIP = 32


def _conv_halo(u, prev_tail, n_taps):
    halo = n_taps - 1
    tail = u[ROW_TILE - halo * SUBLANES:, :]
    sub = lax.broadcasted_iota(jnp.int32, (SUBLANES, u.shape[1]), 0)
    fix = []
    for i in range(halo):
        rows = slice(i * SUBLANES, (i + 1) * SUBLANES)
        fix.append(jnp.where(sub == 0, pltpu.roll(prev_tail[rows], 1, axis=0), pltpu.roll(tail[rows], 1, axis=0)))
    return jnp.concatenate(fix, axis=0), tail


def _conv_strip(u, fix, taps, n_taps, r0):
    halo = n_taps - 1
    out = taps[halo] * u[r0:r0 + CONV_STRIP]
    for k in range(1, halo + 1):
        start = r0 - k * SUBLANES
        if start >= 0:
            shifted = u[start:start + CONV_STRIP]
        else:
            shifted = jnp.concatenate([fix[halo * SUBLANES + start:], u[:start + CONV_STRIP]], axis=0)
        out = out + taps[halo - k] * shifted
    return out


IN_NC = 512
SSM_HALO_ROWS = (SSM_CONV - 1) * SUBLANES


def _in_proj_kernel(x_ref, nw_ref, wz_ref, wx_ref, wdt_ref, cw_ref, cb_ref, z_ref, xbc_ref, dt_ref,
                    perm_ref, out_ref, tail_ref):
    @pl.when(pl.program_id(1) == 0)
    def _():
        tail_ref[...] = jnp.zeros_like(tail_ref)

    h = _rms_normed(x_ref[...], nw_ref[...])
    hb = h.astype(BF16)
    hp = _permute_rows(h, perm_ref).astype(BF16)

    n_x = SSM_CONV_DIM // IN_NC
    z_nc = SSM_D_INNER // n_x

    def chunk_job(i):
        cols = slice(i * IN_NC, (i + 1) * IN_NC)
        zcols = slice(i * z_nc, (i + 1) * z_nc)

        def matmul():
            return (jnp.dot(hp, wx_ref[:, cols], preferred_element_type=F32),
                    jnp.dot(hb, wz_ref[:, zcols], preferred_element_type=F32))

        def epilogue(res):
            u, uz = res
            z_ref[:, zcols] = uz.astype(BF16)
            taps = [cw_ref[k:k + 1, cols] for k in range(SSM_CONV)]
            bias = cb_ref[:, cols]
            fix, tail = _conv_halo(u, tail_ref[:, cols], SSM_CONV)
            tail_ref[:, cols] = tail
            stage = out_ref.at[i % 2]
            for r0 in range(0, ROW_TILE, CONV_STRIP):
                act = _silu(_conv_strip(u, fix, taps, SSM_CONV, r0) + bias)
                for ct in range(IN_NC // LANES):
                    stage[ct, r0:r0 + CONV_STRIP, :] = act[:, ct * LANES:(ct + 1) * LANES]
            for ct in range(IN_NC // LANES):
                c0 = i * IN_NC + ct * LANES
                for s in range(SUBLANES):
                    xbc_ref[s * ROW_GROUPS:(s + 1) * ROW_GROUPS, c0:c0 + LANES] = (
                        stage[ct, pl.ds(s, ROW_GROUPS, stride=SUBLANES), :].astype(BF16))

        return matmul, epilogue

    def dt_job():
        def matmul():
            h_lo = (h - hb.astype(F32)).astype(BF16)
            return jnp.dot(jnp.concatenate([hb, h_lo, hb], axis=1), wdt_ref[...], preferred_element_type=F32)

        def epilogue(u):
            dt_ref[...] = u

        return matmul, epilogue

    _pipelined([chunk_job(i) for i in range(n_x)] + [dt_job()])


def _in_proj(x2, nw, wz, wx, wdt3, cw, cb, bsz, seq):
    nt = seq // ROW_TILE
    rows = lambda b, s: (b * nt + s, 0)
    t = bsz * seq
    return pl.pallas_call(
        _in_proj_kernel,
        grid=(bsz, nt),
        in_specs=[
            pl.BlockSpec((ROW_TILE, D_MODEL), rows),
            _const_spec((1, D_MODEL)),
            _const_spec((D_MODEL, SSM_D_INNER)),
            _const_spec((D_MODEL, SSM_CONV_DIM)),
            _const_spec((3 * D_MODEL, LANES)),
            _const_spec((SSM_CONV, SSM_CONV_DIM)),
            _const_spec((1, SSM_CONV_DIM)),
        ],
        out_specs=[
            pl.BlockSpec((ROW_TILE, SSM_D_INNER), rows),
            pl.BlockSpec((ROW_TILE, SSM_CONV_DIM), rows),
            pl.BlockSpec((ROW_TILE, LANES), rows),
        ],
        out_shape=[
            jax.ShapeDtypeStruct((t, SSM_D_INNER), BF16),
            jax.ShapeDtypeStruct((t, SSM_CONV_DIM), BF16),
            jax.ShapeDtypeStruct((t, LANES), F32),
        ],
        scratch_shapes=[
            pltpu.VMEM((D_MODEL // LANES, ROW_TILE, LANES), F32),
            pltpu.VMEM((2, IN_NC // LANES, ROW_TILE, LANES), F32),
            pltpu.VMEM((SSM_HALO_ROWS, SSM_CONV_DIM), F32),
        ],
        compiler_params=_params(("parallel", "arbitrary")),
        name="in_proj",
    )(x2, nw, wz, wx, wdt3, cw, cb)


def _split_hi_lo(v):
    hi = v.astype(BF16)
    lo = (v - hi.astype(F32)).astype(BF16)
    return jnp.concatenate([hi, lo], axis=1)


SSD_CHUNKS_PER_STEP = 2


def _ssd_kernel(xbc_ref, z_ref, dt_ref, x_ref, dtb_ref, alog_ref, dsk_ref, nw_ref, exp_ref, wout_ref, o_ref,
                y_ref, state_ref):
    L = SSM_CHUNK
    P = SSM_HEAD_DIM

    @pl.when(pl.program_id(1) == 0)
    def _():
        state_ref[...] = jnp.zeros_like(state_ref)

    row = lax.broadcasted_iota(jnp.int32, (L, L), 0)
    col = lax.broadcasted_iota(jnp.int32, (L, L), 1)
    causal = row >= col
    tril = jnp.where(causal, 1.0, 0.0).astype(BF16)
    lane_head = lax.broadcasted_iota(jnp.int32, (L, SSM_GROUP_CH), 1) // P
    head_mask = [jnp.where(lane_head == hh, 1.0, 0.0).astype(BF16) for hh in range(SSM_HEADS_PER_GROUP)]
    neg_a = -jnp.exp(alog_ref[...])

    def decays(rows):
        dt_raw = dt_ref[rows, :] + dtb_ref[...]
        dt = jnp.maximum(dt_raw, 0.0) + _log1p(jnp.exp(-jnp.abs(dt_raw)))
        adt = dt * neg_a
        a_hi = adt.astype(BF16)
        a_r1 = adt - a_hi.astype(F32)
        a_mid = a_r1.astype(BF16)
        a_lo = (a_r1 - a_mid.astype(F32)).astype(BF16)
        a_cs = (jnp.dot(tril, a_hi, preferred_element_type=F32) + jnp.dot(tril, a_mid, preferred_element_type=F32)
                + jnp.dot(tril, a_lo, preferred_element_type=F32))
        a_last = a_cs[L - 1:L, :]
        w_state = dt * jnp.exp(a_last - a_cs)
        e_acs = jnp.exp(a_cs)
        e_last = jnp.broadcast_to(jnp.exp(a_last), (BF16_ROWS, LANES))
        expanded = jnp.dot(_split_hi_lo(jnp.concatenate([dt, w_state, e_acs, e_last], axis=0)), exp_ref[...],
                           preferred_element_type=F32)
        return a_cs, a_cs.T, expanded

    def scan(rows, a_cs, a_cs_t, expanded):
        for g in range(SSM_N_GROUPS):
            xcols = slice(g * SSM_GROUP_CH, (g + 1) * SSM_GROUP_CH)
            x_g = xbc_ref[rows, xcols].astype(F32)
            b_bf = xbc_ref[rows, SSM_D_INNER + g * SSM_D_STATE:SSM_D_INNER + (g + 1) * SSM_D_STATE]
            c_bf = xbc_ref[rows, SSM_D_INNER + SSM_BC_DIM + g * SSM_D_STATE:
                           SSM_D_INNER + SSM_BC_DIM + (g + 1) * SSM_D_STATE]
            cb = lax.dot_general(c_bf, b_bf, (((1,), (1,)), ((), ())), preferred_element_type=F32)
            prev = state_ref[g]
            y_off = jnp.dot(c_bf, prev.astype(BF16), preferred_element_type=F32)
            xdt = (x_g * expanded[0:L, xcols]).astype(BF16)
            xw = (x_g * expanded[L:2 * L, xcols]).astype(BF16)

            lhs, rhs = [], []
            for hh in range(SSM_HEADS_PER_GROUP):
                h = g * SSM_HEADS_PER_GROUP + hh
                seg = jnp.broadcast_to(a_cs[:, h:h + 1], (L, L)) - a_cs_t[h:h + 1, :]
                lhs.append((cb * jnp.exp(jnp.where(causal, seg, NEG_BIG))).astype(BF16))
                rhs.append(xdt * head_mask[hh])
            y_g = jnp.dot(jnp.concatenate(lhs, axis=1), jnp.concatenate(rhs, axis=0), preferred_element_type=F32)
            y_g = y_g + y_off * expanded[2 * L:3 * L, xcols] + x_g * dsk_ref[:, xcols]

            new_state = jnp.dot(b_bf.T, xw, preferred_element_type=F32)
            state_ref[g] = prev * expanded[3 * L:3 * L + 1, xcols] + new_state

            y_g = y_g * _silu(z_ref[rows, xcols].astype(F32))
            y_g = y_g * lax.rsqrt(jnp.mean(y_g * y_g, axis=-1, keepdims=True) + GATED_NORM_EPS)
            y_ref[rows, xcols] = (y_g * nw_ref[:, xcols]).astype(BF16)

    chunks = [slice(ci * L, (ci + 1) * L) for ci in range(SSD_CHUNKS_PER_STEP)]
    pre = [decays(rows) for rows in chunks]
    for rows, args in zip(chunks, pre):
        scan(rows, *args)
        o_ref[rows, :] = x_ref[rows, :] + jnp.dot(y_ref[rows, :], wout_ref[...], preferred_element_type=F32)


def _ssd(xbc, z, dt, x2, dtb, alog, dsk, nw, expand, wout, bsz, seq):
    L = SSD_CHUNKS_PER_STEP * SSM_CHUNK
    nc = seq // L
    rows = lambda b, c: (b * nc + c, 0)
    return pl.pallas_call(
        _ssd_kernel,
        grid=(bsz, nc),
        in_specs=[
            pl.BlockSpec((L, SSM_CONV_DIM), rows),
            pl.BlockSpec((L, SSM_D_INNER), rows),
            pl.BlockSpec((L, LANES), rows),
            pl.BlockSpec((L, D_MODEL), rows),
            _const_spec((1, LANES)),
            _const_spec((1, LANES)),
            _const_spec((1, SSM_D_INNER)),
            _const_spec((1, SSM_D_INNER)),
            _const_spec((2 * LANES, SSM_D_INNER)),
            _const_spec((SSM_D_INNER, D_MODEL)),
        ],
        out_specs=pl.BlockSpec((L, D_MODEL), rows),
        out_shape=jax.ShapeDtypeStruct((bsz * seq, D_MODEL), F32),
        scratch_shapes=[pltpu.VMEM((L, SSM_D_INNER), BF16),
                        pltpu.VMEM((SSM_N_GROUPS, SSM_D_STATE, SSM_GROUP_CH), F32)],
        compiler_params=_params(("parallel", "arbitrary")),
        name="ssd",
    )(xbc, z, dt, x2, dtb, alog, dsk, nw, expand, wout)


FFN_HALO_ROWS = (FFN_CONV - 1) * SUBLANES


def _ffn_kernel(*refs, has_pre, final_norm):
    if has_pre:
        x_ref, pre_ref, wpre_ref, nw_ref, wup_ref, cw_ref, wdn_ref, fnw_ref, o_ref, perm_ref, tail_ref, act_ref = refs
    else:
        x_ref, nw_ref, wup_ref, cw_ref, wdn_ref, fnw_ref, o_ref, perm_ref, tail_ref, act_ref = refs

    @pl.when(pl.program_id(1) == 0)
    def _():
        tail_ref[...] = jnp.zeros_like(tail_ref)

    x1 = x_ref[...]
    if has_pre:
        x1 = x1 + jnp.dot(pre_ref[...], wpre_ref[...], preferred_element_type=F32)
    hp = _regroup_rows(_rms_normed(x1, nw_ref[...]), SUBLANES, ROW_GROUPS).astype(BF16)

    def chunk_job(c):
        def matmul():
            return [jnp.dot(hp, wup_ref[:, half * FFN_DIM + c * FFN_COL:half * FFN_DIM + (c + 1) * FFN_COL],
                            preferred_element_type=F32) for half in range(2)]

        def epilogue(us):
            taps, fixes = [], []
            for half, u in enumerate(us):
                cols = slice(half * FFN_DIM + c * FFN_COL, half * FFN_DIM + (c + 1) * FFN_COL)
                taps.append([cw_ref[k:k + 1, cols] for k in range(FFN_CONV)])
                fix, tail = _conv_halo(u, tail_ref[:, cols], FFN_CONV)
                tail_ref[:, cols] = tail
                fixes.append(fix)
            for r0 in range(0, ROW_TILE, CONV_STRIP):
                gate, val = [_conv_strip(us[half], fixes[half], taps[half], FFN_CONV, r0) for half in range(2)]
                act_ref[r0:r0 + CONV_STRIP, c * FFN_COL:(c + 1) * FFN_COL] = (_silu(gate) * val).astype(BF16)

        return matmul, epilogue

    _pipelined([chunk_job(c) for c in range(FFN_DIM // FFN_COL)])
    acc = jnp.dot(act_ref[...], wdn_ref[...], preferred_element_type=F32)

    out = x1 + _regroup_rows(acc, ROW_GROUPS, SUBLANES)
    if final_norm:
        out = _rms_normed(out, fnw_ref[...])
    o_ref[...] = out


def _ffn(x2, pre, wpre, nw, wup, cw, wdn, fnw, bsz, seq, final_norm):
    nt = seq // ROW_TILE
    rows = lambda b, s: (b * nt + s, 0)
    has_pre = pre is not None
    pre_specs = [pl.BlockSpec((ROW_TILE, pre.shape[1]), rows), _const_spec(wpre.shape)] if has_pre else []
    pre_args = (pre, wpre) if has_pre else ()
    return pl.pallas_call(
        functools.partial(_ffn_kernel, has_pre=has_pre, final_norm=final_norm),
        grid=(bsz, nt),
        in_specs=[pl.BlockSpec((ROW_TILE, D_MODEL), rows)] + pre_specs + [
            _const_spec((1, D_MODEL)),
            _const_spec((D_MODEL, 2 * FFN_DIM)),
            _const_spec((FFN_CONV, 2 * FFN_DIM)),
            _const_spec((FFN_DIM, D_MODEL)),
            _const_spec((1, D_MODEL)),
        ],
        out_specs=pl.BlockSpec((ROW_TILE, D_MODEL), rows),
        out_shape=jax.ShapeDtypeStruct((bsz * seq, D_MODEL), F32),
        scratch_shapes=[
            pltpu.VMEM((D_MODEL // LANES, ROW_TILE, LANES), F32),
            pltpu.VMEM((FFN_HALO_ROWS, 2 * FFN_DIM), F32),
            pltpu.VMEM((ROW_TILE, FFN_DIM), BF16),
        ],
        compiler_params=_params(("parallel", "arbitrary")),
        name="ffn_final" if final_norm else "ffn",
    )(x2, *pre_args, nw, wup, cw, wdn, fnw)


QKV_SUB = ATT_TILE // ROW_TILE
QKV_M = ROW_TILE // ATT_MAX_DIL


def _rotary(v, cos, sin):
    return v * cos + pltpu.roll(v, LANES // 2, axis=1) * sin


def _qkv_kernel(x_ref, kvn_ref, qn_ref, wq_ref, wkv_ref, cos_ref, sin_ref,
                q0_ref, q1_ref, q2_ref, k0_ref, k1_ref, k2_ref, v0_ref, v1_ref, v2_ref, scr_ref):
    tm = ROW_TILE
    for ct in range(D_MODEL // LANES):
        scr_ref[ct] = x_ref[:, ct * LANES:(ct + 1) * LANES]
    x = jnp.concatenate(
        [jnp.concatenate([scr_ref[ct, pl.ds(c, QKV_M, stride=ATT_MAX_DIL), :] for c in range(ATT_MAX_DIL)], axis=0)
         for ct in range(D_MODEL // LANES)], axis=1)
    xn = x * lax.rsqrt(jnp.mean(x * x, axis=-1, keepdims=True) + RMS_EPS)
    h_kv = (xn * kvn_ref[...]).astype(BF16)
    h_q = (xn * qn_ref[...]).astype(BF16)
    cos = cos_ref[...]
    sin = sin_ref[...]
    q_scale = ATT_HEAD_DIM ** -0.5 * math.log2(math.e)

    def emit(res, out_ref, d):
        by_c = [res[c * QKV_M:(c + 1) * QKV_M] for c in range(ATT_MAX_DIL)]
        if d == 16:
            for c in range(ATT_MAX_DIL):
                out_ref[c] = by_c[c].astype(BF16)
        elif d == 4:
            for c in range(ATT_MAX_DIL):
                out_ref[c % 4, c // 4] = by_c[c].astype(BF16)
        else:
            for nl in range(tm // ATT_BLOCK):
                ks = slice(nl * SUBLANES, (nl + 1) * SUBLANES)
                for c in range(0, ATT_MAX_DIL, 2):
                    r0 = nl * ATT_BLOCK + c * SUBLANES
                    out_ref[r0:r0 + BF16_ROWS, :] = jnp.concatenate(
                        [by_c[c][ks], by_c[c + 1][ks]], axis=0).astype(BF16)

    def rot(v, n_heads):
        return jnp.concatenate(
            [_rotary(v[:, hd * LANES:(hd + 1) * LANES], cos, sin) for hd in range(n_heads)], axis=1)

    q_refs = (q0_ref, q1_ref, q2_ref)
    k_refs = (k0_ref, k1_ref, k2_ref)
    v_refs = (v0_ref, v1_ref, v2_ref)
    jobs = []
    for g, d in enumerate(ATT_DILATIONS):
        qc = slice(g * ATT_Q_GROUP_DIM, (g + 1) * ATT_Q_GROUP_DIM)
        kc = slice(g * ATT_KV_GROUP_DIM, (g + 1) * ATT_KV_GROUP_DIM)
        v0 = len(ATT_DILATIONS) * ATT_KV_GROUP_DIM + g * ATT_KV_GROUP_DIM
        vc = slice(v0, v0 + ATT_KV_GROUP_DIM)
        jobs.append((lambda qc=qc: jnp.dot(h_q, wq_ref[:, qc], preferred_element_type=F32),
                     lambda u, g=g, d=d: emit(rot(u, ATT_HEADS) * q_scale, q_refs[g], d)))
        jobs.append((lambda kc=kc: jnp.dot(h_kv, wkv_ref[:, kc], preferred_element_type=F32),
                     lambda u, g=g, d=d: emit(rot(u, ATT_KV_HEADS), k_refs[g], d)))
        jobs.append((lambda vc=vc: jnp.dot(h_kv, wkv_ref[:, vc], preferred_element_type=F32),
                     lambda u, g=g, d=d: emit(u, v_refs[g], d)))
    _pipelined(jobs)


def _qkv(x2, kvn, qn, wq, wkv, cos_t, sin_t, bsz, seq):
    tm = ROW_TILE
    nt = seq // tm
    n_att = seq // ATT_TILE

    def out_arrays(cols):
        shapes, specs = [], []
        for d in ATT_DILATIONS:
            if d == 1:
                shapes.append(jax.ShapeDtypeStruct((bsz * seq, cols), BF16))
                specs.append(pl.BlockSpec((tm, cols), lambda b, s: (b * nt + s, 0)))
            elif d == 4:
                shapes.append(jax.ShapeDtypeStruct((bsz, n_att, 4, QKV_SUB, 4, QKV_M, cols), BF16))
                specs.append(pl.BlockSpec((None, None, 4, None, 4, QKV_M, cols),
                                          lambda b, s: (b, s // QKV_SUB, 0, s % QKV_SUB, 0, 0, 0)))
            else:
                shapes.append(jax.ShapeDtypeStruct((bsz, n_att, d, QKV_SUB, QKV_M, cols), BF16))
                specs.append(pl.BlockSpec((None, None, d, None, QKV_M, cols),
                                          lambda b, s: (b, s // QKV_SUB, 0, s % QKV_SUB, 0, 0)))
        return shapes, specs

    q_shapes, q_specs = out_arrays(ATT_Q_GROUP_DIM)
    k_shapes, k_specs = out_arrays(ATT_KV_GROUP_DIM)
    v_shapes, v_specs = out_arrays(ATT_KV_GROUP_DIM)
    outs = pl.pallas_call(
        _qkv_kernel,
        grid=(bsz, nt),
        in_specs=[
            pl.BlockSpec((tm, D_MODEL), lambda b, s: (b * nt + s, 0)),
            _const_spec((1, D_MODEL)),
            _const_spec((1, D_MODEL)),
            _const_spec(wq.shape),
            _const_spec(wkv.shape),
            pl.BlockSpec((tm, LANES), lambda b, s: (s, 0)),
            pl.BlockSpec((tm, LANES), lambda b, s: (s, 0)),
        ],
        out_specs=q_specs + k_specs + v_specs,
        out_shape=q_shapes + k_shapes + v_shapes,
        scratch_shapes=[pltpu.VMEM((ATT_Q_GROUP_DIM // LANES, tm, LANES), F32)],
        compiler_params=_params(("parallel", "parallel")),
        name="qkv",
    )(x2, kvn, qn, wq, wkv, cos_t, sin_t)
    return [o.reshape(bsz, seq, o.shape[-1]) for o in outs]


ATT_PITCH = ATT_BLOCK + SUBLANES
ATT_UNROLL = ATT_UNITS


def _ext_rows(d):
    return (ATT_UNITS // d + 1) * d * ATT_BLOCK


def _unit_order(g, rho):
    d = ATT_DILATIONS[g]
    if d == 1:
        return (rho % SUBLANES) * ATT_MAX_DIL + rho // SUBLANES
    if d == 4:
        return (rho % QKV_M) * 4 + rho // QKV_M
    return rho


def _state_segments(g, u):
    d = ATT_DILATIONS[g]
    if d == 1:
        return [(pl.multiple_of(c * ATT_PITCH + u * SUBLANES, SUBLANES), c * SUBLANES, SUBLANES)
                for c in range(ATT_MAX_DIL)]
    if d == 4:
        r4, n4 = u // 4, u % 4
        return [(pl.multiple_of((4 * a + r4) * ATT_PITCH + n4 * QKV_M, SUBLANES), a * QKV_M, QKV_M)
                for a in range(4)]
    return [(pl.multiple_of(u * ATT_PITCH, SUBLANES), 0, ATT_BLOCK)]


def _attn_kernel(q0_ref, q1_ref, q2_ref,
                 k0c_ref, k1c_ref, k2c_ref, v0c_ref, v1c_ref, v2c_ref,
                 k0p_ref, k1p_ref, k2p_ref, v0p_ref, v1p_ref, v2p_ref,
                 o_ref, acc_ref, m_ref, l_ref, mask_ref, eye_ref,
                 ke0_ref, ke1_ref, ke2_ref, ve0_ref, ve1_ref, ve2_ref):
    blk = ATT_BLOCK
    no_prev_tile = (pl.program_id(1) == 0).astype(jnp.int32)

    key_row = lax.broadcasted_iota(jnp.int32, (2 * blk, blk), 0)
    qry_row = lax.broadcasted_iota(jnp.int32, (2 * blk, blk), 1)
    in_prev = key_row < blk
    for g in range(len(ATT_DILATIONS)):
        i_s = _unit_order(g, key_row % blk)
        i_q = _unit_order(g, qry_row)
        cur_ok = (~in_prev) & (i_s <= i_q)
        mask_ref[2 * g] = jnp.where((in_prev & (i_s >= i_q)) | cur_ok, 0.0, NEG_BIG).astype(BF16)
        mask_ref[2 * g + 1] = jnp.where(cur_ok, 0.0, NEG_BIG).astype(BF16)
    e_row = lax.broadcasted_iota(jnp.int32, (ATT_REP * blk, blk), 0)
    e_col = lax.broadcasted_iota(jnp.int32, (ATT_REP * blk, blk), 1)
    eye_ref[...] = jnp.where(e_row % blk == e_col, 1.0, 0.0).astype(BF16)

    groups = ((q0_ref, k0c_ref, v0c_ref, k0p_ref, v0p_ref, ke0_ref, ve0_ref),
              (q1_ref, k1c_ref, v1c_ref, k1p_ref, v1p_ref, ke1_ref, ve1_ref),
              (q2_ref, k2c_ref, v2c_ref, k2p_ref, v2p_ref, ke2_ref, ve2_ref))

    for g, d in enumerate(ATT_DILATIONS):
        _, kc_ref, vc_ref, kp_ref, vp_ref, ke_ref, ve_ref = groups[g]
        per_res = ATT_UNITS // d
        ve_ref[:, ATT_HEAD_DIM:] = jnp.ones((_ext_rows(d), ATT_HEAD_DIM), BF16)
        for r in range(d):
            dst = r * (per_res + 1) * blk
            src_prev = (r * per_res + per_res - 1) * blk
            src = r * per_res * blk
            ke_ref[dst:dst + blk, :] = kp_ref[src_prev:src_prev + blk, :]
            ve_ref[dst:dst + blk, 0:ATT_HEAD_DIM] = vp_ref[src_prev:src_prev + blk, :]
            ke_ref[dst + blk:dst + (per_res + 1) * blk, :] = kc_ref[src:src + per_res * blk, :]
            ve_ref[dst + blk:dst + (per_res + 1) * blk, 0:ATT_HEAD_DIM] = vc_ref[src:src + per_res * blk, :]

    def unit(u, g, q_ref, ke_ref, ve_ref):
        d = ATT_DILATIONS[g]
        per_res = ATT_UNITS // d
        r = u // per_res
        n = u % per_res
        r0 = pl.multiple_of(u * blk, blk)
        e0 = pl.multiple_of((r * (per_res + 1) + n) * blk, blk)
        q = jnp.concatenate([q_ref[pl.ds(r0, blk), hd * LANES:(hd + 1) * LANES] for hd in range(ATT_REP)], axis=0)
        q_aug = jnp.concatenate([q, eye_ref[...]], axis=1)
        k_aug = jnp.concatenate([ke_ref[pl.ds(e0, 2 * blk), :],
                                 mask_ref[2 * g + jnp.where(n == 0, no_prev_tile, 0)]], axis=1)
        s = lax.dot_general(q_aug, k_aug, (((1,), (1,)), ((), ())), preferred_element_type=F32)
        m_cur = jnp.max(s, axis=-1, keepdims=True)
        p = jnp.exp2(s - m_cur).astype(BF16)
        pv = jnp.dot(p, ve_ref[pl.ds(e0, 2 * blk), :], preferred_element_type=F32)
        acc = pv[:, :ATT_HEAD_DIM]
        l_new = pv[:, ATT_HEAD_DIM:]
        m_new = jnp.broadcast_to(m_cur, (ATT_REP * blk, LANES))
        segs = _state_segments(g, u)

        def load(ref):
            return jnp.concatenate([ref[hd, pl.ds(s0, n_rows), :] for hd in range(ATT_REP)
                                    for (s0, _, n_rows) in segs], axis=0)

        if g > 0:
            m_old, l_old, a_old = load(m_ref), load(l_ref), load(acc_ref)
            m_c = m_new
            m_new = jnp.maximum(m_old, m_c)
            w_old = jnp.exp2(m_old - m_new)
            w_cur = jnp.exp2(m_c - m_new)
            l_new = w_old * l_old + w_cur * l_new
            acc = w_old * a_old + w_cur * acc
        for hd in range(ATT_REP):
            for (s0, v0, n_rows) in segs:
                rows = slice(hd * blk + v0, hd * blk + v0 + n_rows)
                m_ref[hd, pl.ds(s0, n_rows), :] = m_new[rows]
                l_ref[hd, pl.ds(s0, n_rows), :] = l_new[rows]
                acc_ref[hd, pl.ds(s0, n_rows), :] = acc[rows]

    for g in range(len(ATT_DILATIONS)):
        q_ref, _, _, _, _, ke_ref, ve_ref = groups[g]

        def body(u, carry, g=g, q_ref=q_ref, ke_ref=ke_ref, ve_ref=ve_ref):
            unit(u, g, q_ref, ke_ref, ve_ref)
            return carry

        lax.fori_loop(0, ATT_UNITS, body, 0, unroll=ATT_UNROLL)

    def finish(m, carry):
        p0 = pl.multiple_of(m * ATT_MAX_DIL, ATT_MAX_DIL)
        for hd in range(ATT_REP):
            halves = []
            for j in range(ATT_MAX_DIL // SUBLANES):
                idx = pl.ds(j * SUBLANES * ATT_PITCH + m, SUBLANES, stride=ATT_PITCH)
                halves.append(acc_ref[hd, idx, :] / l_ref[hd, idx, :])
            o_ref[pl.ds(p0, ATT_MAX_DIL), hd * LANES:(hd + 1) * LANES] = jnp.concatenate(halves, axis=0).astype(BF16)
        return carry

    lax.fori_loop(0, ATT_BLOCK, finish, 0, unroll=ATT_UNROLL)


def _attn(qs, ks, vs, bsz, seq):
    n_att = seq // ATT_TILE
    cur = lambda b, i, j: (b, i, j)
    prev = lambda b, i, j: (b, jnp.maximum(i - 1, 0), j)
    q_spec = pl.BlockSpec((None, ATT_TILE, ATT_REP * ATT_HEAD_DIM), cur)
    kv_cur = pl.BlockSpec((None, ATT_TILE, ATT_HEAD_DIM), cur)
    kv_prev = pl.BlockSpec((None, ATT_TILE, ATT_HEAD_DIM), prev)
    width = ATT_REP * ATT_HEAD_DIM
    stat = pltpu.VMEM((ATT_REP, ATT_MAX_DIL * ATT_PITCH, ATT_HEAD_DIM), F32)
    return pl.pallas_call(
        _attn_kernel,
        grid=(bsz, n_att, ATT_KV_HEADS),
        in_specs=[q_spec] * 3 + [kv_cur] * 6 + [kv_prev] * 6,
        out_specs=pl.BlockSpec((None, ATT_TILE, width), cur),
        out_shape=jax.ShapeDtypeStruct((bsz, seq, ATT_HEADS * ATT_HEAD_DIM), BF16),
        scratch_shapes=[stat, stat, stat,
                        pltpu.VMEM((2 * len(ATT_DILATIONS), 2 * ATT_BLOCK, ATT_BLOCK), BF16),
                        pltpu.VMEM((ATT_REP * ATT_BLOCK, ATT_BLOCK), BF16)]
        + [pltpu.VMEM((_ext_rows(d), ATT_HEAD_DIM), BF16) for d in ATT_DILATIONS]
        + [pltpu.VMEM((_ext_rows(d), 2 * ATT_HEAD_DIM), BF16) for d in ATT_DILATIONS],
        compiler_params=_params(("parallel", "parallel", "parallel")),
        name="attn",
    )(*qs, *ks, *vs, *ks, *vs)


def _permute_head_dims(w, n_heads):
    half = ROPE_DIM // 2
    n_low = LANES // 2 - half
    w = w.reshape(w.shape[0], n_heads, ATT_HEAD_DIM)
    w = jnp.concatenate([w[..., :half], w[..., ROPE_DIM:ROPE_DIM + n_low], w[..., half:ROPE_DIM],
                         w[..., ROPE_DIM + n_low:]], axis=-1)
    return w.reshape(w.shape[0], n_heads * ATT_HEAD_DIM)


def _rope_tables(seq):
    half = ROPE_DIM // 2
    inv_freq = jnp.power(jnp.float32(ROPE_THETA), -jnp.arange(0, ROPE_DIM, 2, dtype=F32) / ROPE_DIM)
    ang = jnp.arange(seq, dtype=jnp.int32).astype(F32)[:, None] * inv_freq[None, :]
    cos, sin = jnp.cos(ang), jnp.sin(ang)
    ones = jnp.ones((seq, LANES // 2 - half), F32)
    zeros = jnp.zeros_like(ones)
    cos_t = jnp.concatenate([cos, ones, cos, ones], axis=1)
    sin_t = jnp.concatenate([-sin, zeros, sin, zeros], axis=1)
    regroup = lambda tbl: tbl.reshape(
        seq // ROW_TILE, QKV_M, ATT_MAX_DIL, LANES).transpose(0, 2, 1, 3).reshape(seq, LANES)
    return regroup(cos_t), regroup(sin_t)


def _head_expand():
    head = jnp.arange(LANES)[:, None]
    chan_head = (jnp.arange(SSM_D_INNER) // SSM_HEAD_DIM)[None, :]
    e = (head == chan_head).astype(BF16)
    return jnp.concatenate([e, e], axis=0)


def _dt_weights(w):
    w = jnp.pad(w, ((0, 0), (0, LANES - SSM_N_HEADS)))
    hi = w.astype(BF16)
    lo = (w - hi.astype(F32)).astype(BF16)
    return jnp.concatenate([hi, hi, lo], axis=0)


def kernel(x, a_norm, ssm_w_in, ssm_conv_w, ssm_conv_b, ssm_dt_bias, ssm_a_log, ssm_d, ssm_norm, ssm_w_out,
           kv_norm, w_kv, b_norm, att_w_q, att_w_o, ffn_norm, ffn_w_up, ffn_conv_w, ffn_w_down, final_norm):
    bsz, seq, dm = x.shape
    assert dm == D_MODEL and seq % ATT_TILE == 0
    assert a_norm.shape[0] == 1 and b_norm.shape[0] == 1 and ffn_norm.shape[0] == 2
    t = bsz * seq
    x2 = x.reshape(t, dm)
    row = lambda v: v.reshape(1, -1).astype(F32)

    w_in = ssm_w_in[0]
    wz = w_in[:, :SSM_D_INNER].astype(BF16)
    wx = w_in[:, SSM_D_INNER:SSM_D_INNER + SSM_CONV_DIM].astype(BF16)
    wdt3 = _dt_weights(w_in[:, SSM_D_INNER + SSM_CONV_DIM:])
    pad_heads = lambda v: jnp.pad(row(v), ((0, 0), (0, LANES - SSM_N_HEADS)))
    z, xbc, dt = _in_proj(x2, row(a_norm[0]), wz, wx, wdt3, ssm_conv_w[0], row(ssm_conv_b[0]), bsz, seq)
    x2 = _ssd(xbc, z, dt, x2, pad_heads(ssm_dt_bias[0]), pad_heads(ssm_a_log[0]),
              row(jnp.repeat(ssm_d[0], SSM_HEAD_DIM)), row(ssm_norm[0]), _head_expand(), ssm_w_out[0].astype(BF16),
              bsz, seq)
    x2 = _ffn(x2, None, None, row(ffn_norm[0]), ffn_w_up[0].astype(BF16), ffn_conv_w[0],
              ffn_w_down[0].astype(BF16), row(final_norm), bsz, seq, False)

    cos_t, sin_t = _rope_tables(seq)
    n_q_heads = len(ATT_DILATIONS) * ATT_HEADS
    n_k_heads = len(ATT_DILATIONS) * ATT_KV_HEADS
    k_cols = n_k_heads * ATT_HEAD_DIM
    wq = _permute_head_dims(att_w_q[0], n_q_heads).astype(BF16)
    wkv = jnp.concatenate([_permute_head_dims(w_kv[:, :k_cols], n_k_heads), w_kv[:, k_cols:]], axis=1).astype(BF16)
    outs = _qkv(x2, row(kv_norm), row(b_norm[0]), wq, wkv, cos_t, sin_t, bsz, seq)
    o = _attn(outs[0:3], outs[3:6], outs[6:9], bsz, seq)
    x2 = _ffn(x2, o.reshape(t, dm), att_w_o[0].astype(BF16), row(ffn_norm[1]), ffn_w_up[1].astype(BF16),
              ffn_conv_w[1], ffn_w_down[1].astype(BF16), row(final_norm), bsz, seq, True)
    return x2.reshape(bsz, seq, dm)
```

```python
import functools
import math

import jax
import jax.numpy as jnp
from jax import lax
from jax.experimental import pallas as pl
from jax.experimental.pallas import tpu as pltpu

F32 = jnp.float32
BF16 = jnp.bfloat16

D_MODEL = 1024
RMS_EPS = 1e-6
GATED_NORM_EPS = 1e-5

SSM_D_INNER = 2048
SSM_HEAD_DIM = 64
SSM_N_HEADS = 32
SSM_N_GROUPS = 8
SSM_HEADS_PER_GROUP = SSM_N_HEADS // SSM_N_GROUPS
SSM_GROUP_CH = SSM_D_INNER // SSM_N_GROUPS
SSM_D_STATE = 128
SSM_CONV = 4
SSM_CHUNK = 128
SSM_BC_DIM = SSM_N_GROUPS * SSM_D_STATE
SSM_CONV_DIM = SSM_D_INNER + 2 * SSM_BC_DIM

ATT_DILATIONS = (1, 4, 16)
ATT_MAX_DIL = max(ATT_DILATIONS)
ATT_BLOCK = 128
ATT_HEAD_DIM = 128
ATT_HEADS = 8
ATT_KV_HEADS = 2
ATT_REP = ATT_HEADS // ATT_KV_HEADS
ATT_TILE = ATT_BLOCK * ATT_MAX_DIL
ATT_UNITS = ATT_TILE // ATT_BLOCK
ATT_Q_GROUP_DIM = ATT_HEADS * ATT_HEAD_DIM
ATT_KV_GROUP_DIM = ATT_KV_HEADS * ATT_HEAD_DIM
ROPE_DIM = 32
ROPE_THETA = 500000.0

FFN_DIM = 2816
FFN_CONV = 3
FFN_COL = 256

LANES = 128
SUBLANES = 8
BF16_ROWS = 2 * SUBLANES
NEG_BIG = -1e30
VMEM_LIMIT = 56 * 1024 * 1024

ROW_TILE = 512
ROW_GROUPS = ROW_TILE // SUBLANES


def _const_spec(shape):
    nd = len(shape)
    return pl.BlockSpec(shape, lambda *_: (0,) * nd, pipeline_mode=pl.Buffered(1))


def _params(sem):
    return pltpu.CompilerParams(dimension_semantics=sem, vmem_limit_bytes=VMEM_LIMIT)


def _silu(v):
    h = 0.5 * v
    return h + h * jnp.tanh(h)


def _log1p(v):
    u = 1.0 + v
    return jnp.where(u == 1.0, v, jnp.log(u) * (v / (u - 1.0)))


def _rms_normed(x, w):
    return x * lax.rsqrt(jnp.mean(x * x, axis=-1, keepdims=True) + RMS_EPS) * w


def _pipelined(jobs):
    pending = jobs[0][0]()
    for i, (_, epilogue) in enumerate(jobs):
        upcoming = jobs[i + 1][0]() if i + 1 < len(jobs) else None
        epilogue(pending)
        pending = upcoming


def _permute_rows(v, scr_ref):
    n_ct = v.shape[1] // LANES
    for ct in range(n_ct):
        for s in range(SUBLANES):
            scr_ref[ct, pl.ds(s, ROW_GROUPS, stride=SUBLANES), :] = (
                v[s * ROW_GROUPS:(s + 1) * ROW_GROUPS, ct * LANES:(ct + 1) * LANES])
    return jnp.concatenate([scr_ref[ct] for ct in range(n_ct)], axis=1)


def _unpermute_rows(v, scr_ref, emit):
    n_ct = v.shape[1] // LANES
    for ct in range(n_ct):
        scr_ref[ct] = v[:, ct * LANES:(ct + 1) * LANES]
    for ct in range(n_ct):
        for s in range(SUBLANES):
            emit(s, ct, scr_ref[ct, pl.ds(s, ROW_GROUPS, stride=SUBLANES), :])


CONV_STRIP = 32


def _conv_halo(u, prev_tail, n_taps):
    halo = n_taps - 1
    tail = u[ROW_TILE - halo * SUBLANES:, :]
    sub = lax.broadcasted_iota(jnp.int32, (SUBLANES, u.shape[1]), 0)
    fix = []
    for i in range(halo):
        rows = slice(i * SUBLANES, (i + 1) * SUBLANES)
        fix.append(jnp.where(sub == 0, pltpu.roll(prev_tail[rows], 1, axis=0), pltpu.roll(tail[rows], 1, axis=0)))
    return jnp.concatenate(fix, axis=0), tail


def _conv_strip(u, fix, taps, n_taps, r0):
    halo = n_taps - 1
    out = taps[halo] * u[r0:r0 + CONV_STRIP]
    for k in range(1, halo + 1):
        start = r0 - k * SUBLANES
        if start >= 0:
            shifted = u[start:start + CONV_STRIP]
        else:
            shifted = jnp.concatenate([fix[halo * SUBLANES + start:], u[:start + CONV_STRIP]], axis=0)
        out = out + taps[halo - k] * shifted
    return out


IN_NC = 512
SSM_HALO_ROWS = (SSM_CONV - 1) * SUBLANES


def _in_proj_kernel(x_ref, nw_ref, wz_ref, wx_ref, wdt_ref, cw_ref, cb_ref, z_ref, xbc_ref, dt_ref,
                    perm_ref, out_ref, tail_ref):
    @pl.when(pl.program_id(1) == 0)
    def _():
        tail_ref[...] = jnp.zeros_like(tail_ref)

    h = _rms_normed(x_ref[...], nw_ref[...])
    hb = h.astype(BF16)
    hp = _permute_rows(h, perm_ref).astype(BF16)

    n_x = SSM_CONV_DIM // IN_NC
    z_nc = SSM_D_INNER // n_x

    def chunk_job(i):
        cols = slice(i * IN_NC, (i + 1) * IN_NC)
        zcols = slice(i * z_nc, (i + 1) * z_nc)

        def matmul():
            return (jnp.dot(hp, wx_ref[:, cols], preferred_element_type=F32),
                    jnp.dot(hb, wz_ref[:, zcols], preferred_element_type=F32))

        def epilogue(res):
            u, uz = res
            z_ref[:, zcols] = uz.astype(BF16)
            taps = [cw_ref[k:k + 1, cols] for k in range(SSM_CONV)]
            bias = cb_ref[:, cols]
            fix, tail = _conv_halo(u, tail_ref[:, cols], SSM_CONV)
            tail_ref[:, cols] = tail
            stage = out_ref.at[i % 2]
            for r0 in range(0, ROW_TILE, CONV_STRIP):
                act = _silu(_conv_strip(u, fix, taps, SSM_CONV, r0) + bias)
                for ct in range(IN_NC // LANES):
                    stage[ct, r0:r0 + CONV_STRIP, :] = act[:, ct * LANES:(ct + 1) * LANES]
            for ct in range(IN_NC // LANES):
                c0 = i * IN_NC + ct * LANES
                for s in range(SUBLANES):
                    xbc_ref[s * ROW_GROUPS:(s + 1) * ROW_GROUPS, c0:c0 + LANES] = (
                        stage[ct, pl.ds(s, ROW_GROUPS, stride=SUBLANES), :].astype(BF16))

        return matmul, epilogue

    def dt_job():
        def matmul():
            h_lo = (h - hb.astype(F32)).astype(BF16)
            return jnp.dot(jnp.concatenate([hb, h_lo, hb], axis=1), wdt_ref[...], preferred_element_type=F32)

        def epilogue(u):
            dt_ref[...] = u

        return matmul, epilogue

    _pipelined([chunk_job(i) for i in range(n_x)] + [dt_job()])


def _in_proj(x2, nw, wz, wx, wdt3, cw, cb, bsz, seq):
    nt = seq // ROW_TILE
    rows = lambda b, s: (b * nt + s, 0)
    t = bsz * seq
    return pl.pallas_call(
        _in_proj_kernel,
        grid=(bsz, nt),
        in_specs=[
            pl.BlockSpec((ROW_TILE, D_MODEL), rows),
            _const_spec((1, D_MODEL)),
            _const_spec((D_MODEL, SSM_D_INNER)),
            _const_spec((D_MODEL, SSM_CONV_DIM)),
            _const_spec((3 * D_MODEL, LANES)),
            _const_spec((SSM_CONV, SSM_CONV_DIM)),
            _const_spec((1, SSM_CONV_DIM)),
        ],
        out_specs=[
            pl.BlockSpec((ROW_TILE, SSM_D_INNER), rows),
            pl.BlockSpec((ROW_TILE, SSM_CONV_DIM), rows),
            pl.BlockSpec((ROW_TILE, LANES), rows),
        ],
        out_shape=[
            jax.ShapeDtypeStruct((t, SSM_D_INNER), BF16),
            jax.ShapeDtypeStruct((t, SSM_CONV_DIM), BF16),
            jax.ShapeDtypeStruct((t, LANES), F32),
        ],
        scratch_shapes=[
            pltpu.VMEM((D_MODEL // LANES, ROW_TILE, LANES), F32),
            pltpu.VMEM((2, IN_NC // LANES, ROW_TILE, LANES), F32),
            pltpu.VMEM((SSM_HALO_ROWS, SSM_CONV_DIM), F32),
        ],
        compiler_params=_params(("parallel", "arbitrary")),
        name="in_proj",
    )(x2, nw, wz, wx, wdt3, cw, cb)


def _split_hi_lo(v):
    hi = v.astype(BF16)
    lo = (v - hi.astype(F32)).astype(BF16)
    return jnp.concatenate([hi, lo], axis=1)


SSD_CHUNKS_PER_STEP = 2


def _ssd_kernel(xbc_ref, z_ref, dt_ref, x_ref, dtb_ref, alog_ref, dsk_ref, nw_ref, exp_ref, wout_ref, o_ref,
                y_ref, state_ref):
    L = SSM_CHUNK
    P = SSM_HEAD_DIM

    @pl.when(pl.program_id(1) == 0)
    def _():
        state_ref[...] = jnp.zeros_like(state_ref)

    row = lax.broadcasted_iota(jnp.int32, (L, L), 0)
    col = lax.broadcasted_iota(jnp.int32, (L, L), 1)
    causal = row >= col
    tril = jnp.where(causal, 1.0, 0.0).astype(BF16)
    lane_head = lax.broadcasted_iota(jnp.int32, (L, SSM_GROUP_CH), 1) // P
    head_mask = [jnp.where(lane_head == hh, 1.0, 0.0).astype(BF16) for hh in range(SSM_HEADS_PER_GROUP)]
    neg_a = -jnp.exp(alog_ref[...])

    def decays(rows):
        dt_raw = dt_ref[rows, :] + dtb_ref[...]
        dt = jnp.maximum(dt_raw, 0.0) + _log1p(jnp.exp(-jnp.abs(dt_raw)))
        adt = dt * neg_a
        a_hi = adt.astype(BF16)
        a_r1 = adt - a_hi.astype(F32)
        a_mid = a_r1.astype(BF16)
        a_lo = (a_r1 - a_mid.astype(F32)).astype(BF16)
        a_cs = (jnp.dot(tril, a_hi, preferred_element_type=F32) + jnp.dot(tril, a_mid, preferred_element_type=F32)
                + jnp.dot(tril, a_lo, preferred_element_type=F32))
        a_last = a_cs[L - 1:L, :]
        w_state = dt * jnp.exp(a_last - a_cs)
        e_acs = jnp.exp(a_cs)
        e_last = jnp.broadcast_to(jnp.exp(a_last), (BF16_ROWS, LANES))
        expanded = jnp.dot(_split_hi_lo(jnp.concatenate([dt, w_state, e_acs, e_last], axis=0)), exp_ref[...],
                           preferred_element_type=F32)
        return a_cs, a_cs.T, expanded

    def scan(rows, a_cs, a_cs_t, expanded):
        for g in range(SSM_N_GROUPS):
            xcols = slice(g * SSM_GROUP_CH, (g + 1) * SSM_GROUP_CH)
            x_g = xbc_ref[rows, xcols].astype(F32)
            b_bf = xbc_ref[rows, SSM_D_INNER + g * SSM_D_STATE:SSM_D_INNER + (g + 1) * SSM_D_STATE]
            c_bf = xbc_ref[rows, SSM_D_INNER + SSM_BC_DIM + g * SSM_D_STATE:
                           SSM_D_INNER + SSM_BC_DIM + (g + 1) * SSM_D_STATE]
            cb = lax.dot_general(c_bf, b_bf, (((1,), (1,)), ((), ())), preferred_element_type=F32)
            prev = state_ref[g]
            y_off = jnp.dot(c_bf, prev.astype(BF16), preferred_element_type=F32)
            xdt = (x_g * expanded[0:L, xcols]).astype(BF16)
            xw = (x_g * expanded[L:2 * L, xcols]).astype(BF16)

            lhs, rhs = [], []
            for hh in range(SSM_HEADS_PER_GROUP):
                h = g * SSM_HEADS_PER_GROUP + hh
                seg = jnp.broadcast_to(a_cs[:, h:h + 1], (L, L)) - a_cs_t[h:h + 1, :]
                lhs.append((cb * jnp.exp(jnp.where(causal, seg, NEG_BIG))).astype(BF16))
                rhs.append(xdt * head_mask[hh])
            y_g = jnp.dot(jnp.concatenate(lhs, axis=1), jnp.concatenate(rhs, axis=0), preferred_element_type=F32)
            y_g = y_g + y_off * expanded[2 * L:3 * L, xcols] + x_g * dsk_ref[:, xcols]

            new_state = jnp.dot(b_bf.T, xw, preferred_element_type=F32)
            state_ref[g] = prev * expanded[3 * L:3 * L + 1, xcols] + new_state

            y_g = y_g * _silu(z_ref[rows, xcols].astype(F32))
            y_g = y_g * lax.rsqrt(jnp.mean(y_g * y_g, axis=-1, keepdims=True) + GATED_NORM_EPS)
            y_ref[rows, xcols] = (y_g * nw_ref[:, xcols]).astype(BF16)

    chunks = [slice(ci * L, (ci + 1) * L) for ci in range(SSD_CHUNKS_PER_STEP)]
    pre = [decays(rows) for rows in chunks]
    for rows, args in zip(chunks, pre):
        scan(rows, *args)
        o_ref[rows, :] = x_ref[rows, :] + jnp.dot(y_ref[rows, :], wout_ref[...], preferred_element_type=F32)


def _ssd(xbc, z, dt, x2, dtb, alog, dsk, nw, expand, wout, bsz, seq):
    L = SSD_CHUNKS_PER_STEP * SSM_CHUNK
    nc = seq // L
    rows = lambda b, c: (b * nc + c, 0)
    return pl.pallas_call(
        _ssd_kernel,
        grid=(bsz, nc),
        in_specs=[
            pl.BlockSpec((L, SSM_CONV_DIM), rows),
            pl.BlockSpec((L, SSM_D_INNER), rows),
            pl.BlockSpec((L, LANES), rows),
            pl.BlockSpec((L, D_MODEL), rows),
            _const_spec((1, LANES)),
            _const_spec((1, LANES)),
            _const_spec((1, SSM_D_INNER)),
            _const_spec((1, SSM_D_INNER)),
            _const_spec((2 * LANES, SSM_D_INNER)),
            _const_spec((SSM_D_INNER, D_MODEL)),
        ],
        out_specs=pl.BlockSpec((L, D_MODEL), rows),
        out_shape=jax.ShapeDtypeStruct((bsz * seq, D_MODEL), F32),
        scratch_shapes=[pltpu.VMEM((L, SSM_D_INNER), BF16),
                        pltpu.VMEM((SSM_N_GROUPS, SSM_D_STATE, SSM_GROUP_CH), F32)],
        compiler_params=_params(("parallel", "arbitrary")),
        name="ssd",
    )(xbc, z, dt, x2, dtb, alog, dsk, nw, expand, wout)


FFN_HALO_ROWS = (FFN_CONV - 1) * SUBLANES


def _ffn_kernel(*refs, has_pre, final_norm):
    if has_pre:
        x_ref, pre_ref, wpre_ref, nw_ref, wup_ref, cw_ref, wdn_ref, fnw_ref, o_ref, perm_ref, tail_ref, act_ref = refs
    else:
        x_ref, nw_ref, wup_ref, cw_ref, wdn_ref, fnw_ref, o_ref, perm_ref, tail_ref, act_ref = refs

    @pl.when(pl.program_id(1) == 0)
    def _():
        tail_ref[...] = jnp.zeros_like(tail_ref)

    x1 = x_ref[...]
    if has_pre:
        x1 = x1 + jnp.dot(pre_ref[...], wpre_ref[...], preferred_element_type=F32)
    hp = _permute_rows(_rms_normed(x1, nw_ref[...]), perm_ref).astype(BF16)

    def chunk_job(c):
        def matmul():
            return [jnp.dot(hp, wup_ref[:, half * FFN_DIM + c * FFN_COL:half * FFN_DIM + (c + 1) * FFN_COL],
                            preferred_element_type=F32) for half in range(2)]

        def epilogue(us):
            taps, fixes = [], []
            for half, u in enumerate(us):
                cols = slice(half * FFN_DIM + c * FFN_COL, half * FFN_DIM + (c + 1) * FFN_COL)
                taps.append([cw_ref[k:k + 1, cols] for k in range(FFN_CONV)])
                fix, tail = _conv_halo(u, tail_ref[:, cols], FFN_CONV)
                tail_ref[:, cols] = tail
                fixes.append(fix)
            for r0 in range(0, ROW_TILE, CONV_STRIP):
                gate, val = [_conv_strip(us[half], fixes[half], taps[half], FFN_CONV, r0) for half in range(2)]
                act_ref[r0:r0 + CONV_STRIP, c * FFN_COL:(c + 1) * FFN_COL] = (_silu(gate) * val).astype(BF16)

        return matmul, epilogue

    _pipelined([chunk_job(c) for c in range(FFN_DIM // FFN_COL)])
    acc = jnp.dot(act_ref[...], wdn_ref[...], preferred_element_type=F32)

    def emit(s, ct, block):
        o_ref[s * ROW_GROUPS:(s + 1) * ROW_GROUPS, ct * LANES:(ct + 1) * LANES] = block

    _unpermute_rows(acc, perm_ref, emit)
    out = x1 + o_ref[...]
    if final_norm:
        out = _rms_normed(out, fnw_ref[...])
    o_ref[...] = out


def _ffn(x2, pre, wpre, nw, wup, cw, wdn, fnw, bsz, seq, final_norm):
    nt = seq // ROW_TILE
    rows = lambda b, s: (b * nt + s, 0)
    has_pre = pre is not None
    pre_specs = [pl.BlockSpec((ROW_TILE, pre.shape[1]), rows), _const_spec(wpre.shape)] if has_pre else []
    pre_args = (pre, wpre) if has_pre else ()
    return pl.pallas_call(
        functools.partial(_ffn_kernel, has_pre=has_pre, final_norm=final_norm),
        grid=(bsz, nt),
        in_specs=[pl.BlockSpec((ROW_TILE, D_MODEL), rows)] + pre_specs + [
            _const_spec((1, D_MODEL)),
            _const_spec((D_MODEL, 2 * FFN_DIM)),
            _const_spec((FFN_CONV, 2 * FFN_DIM)),
            _const_spec((FFN_DIM, D_MODEL)),
            _const_spec((1, D_MODEL)),
        ],
        out_specs=pl.BlockSpec((ROW_TILE, D_MODEL), rows),
        out_shape=jax.ShapeDtypeStruct((bsz * seq, D_MODEL), F32),
        scratch_shapes=[
            pltpu.VMEM((D_MODEL // LANES, ROW_TILE, LANES), F32),
            pltpu.VMEM((FFN_HALO_ROWS, 2 * FFN_DIM), F32),
            pltpu.VMEM((ROW_TILE, FFN_DIM), BF16),
        ],
        compiler_params=_params(("parallel", "arbitrary")),
        name="ffn_final" if final_norm else "ffn",
    )(x2, *pre_args, nw, wup, cw, wdn, fnw)


QKV_SUB = ATT_TILE // ROW_TILE
QKV_M = ROW_TILE // ATT_MAX_DIL


def _rotary(v, cos, sin):
    return v * cos + pltpu.roll(v, LANES // 2, axis=1) * sin


def _qkv_kernel(x_ref, kvn_ref, qn_ref, wq_ref, wkv_ref, cos_ref, sin_ref,
                q0_ref, q1_ref, q2_ref, k0_ref, k1_ref, k2_ref, v0_ref, v1_ref, v2_ref, scr_ref):
    tm = ROW_TILE
    for ct in range(D_MODEL // LANES):
        scr_ref[ct] = x_ref[:, ct * LANES:(ct + 1) * LANES]
    x = jnp.concatenate(
        [jnp.concatenate([scr_ref[ct, pl.ds(c, QKV_M, stride=ATT_MAX_DIL), :] for c in range(ATT_MAX_DIL)], axis=0)
         for ct in range(D_MODEL // LANES)], axis=1)
    xn = x * lax.rsqrt(jnp.mean(x * x, axis=-1, keepdims=True) + RMS_EPS)
    h_kv = (xn * kvn_ref[...]).astype(BF16)
    h_q = (xn * qn_ref[...]).astype(BF16)
    cos = cos_ref[...]
    sin = sin_ref[...]
    q_scale = ATT_HEAD_DIM ** -0.5 * math.log2(math.e)

    def emit(res, out_ref, d):
        by_c = [res[c * QKV_M:(c + 1) * QKV_M] for c in range(ATT_MAX_DIL)]
        if d == 16:
            for c in range(ATT_MAX_DIL):
                out_ref[c] = by_c[c].astype(BF16)
        elif d == 4:
            for c in range(ATT_MAX_DIL):
                out_ref[c % 4, c // 4] = by_c[c].astype(BF16)
        else:
            for nl in range(tm // ATT_BLOCK):
                ks = slice(nl * SUBLANES, (nl + 1) * SUBLANES)
                for c in range(0, ATT_MAX_DIL, 2):
                    r0 = nl * ATT_BLOCK + c * SUBLANES
                    out_ref[r0:r0 + BF16_ROWS, :] = jnp.concatenate(
                        [by_c[c][ks], by_c[c + 1][ks]], axis=0).astype(BF16)

    def rot(v, n_heads):
        return jnp.concatenate(
            [_rotary(v[:, hd * LANES:(hd + 1) * LANES], cos, sin) for hd in range(n_heads)], axis=1)

    q_refs = (q0_ref, q1_ref, q2_ref)
    k_refs = (k0_ref, k1_ref, k2_ref)
    v_refs = (v0_ref, v1_ref, v2_ref)
    jobs = []
    for g, d in enumerate(ATT_DILATIONS):
        qc = slice(g * ATT_Q_GROUP_DIM, (g + 1) * ATT_Q_GROUP_DIM)
        kc = slice(g * ATT_KV_GROUP_DIM, (g + 1) * ATT_KV_GROUP_DIM)
        v0 = len(ATT_DILATIONS) * ATT_KV_GROUP_DIM + g * ATT_KV_GROUP_DIM
        vc = slice(v0, v0 + ATT_KV_GROUP_DIM)
        jobs.append((lambda qc=qc: jnp.dot(h_q, wq_ref[:, qc], preferred_element_type=F32),
                     lambda u, g=g, d=d: emit(rot(u, ATT_HEADS) * q_scale, q_refs[g], d)))
        jobs.append((lambda kc=kc: jnp.dot(h_kv, wkv_ref[:, kc], preferred_element_type=F32),
                     lambda u, g=g, d=d: emit(rot(u, ATT_KV_HEADS), k_refs[g], d)))
        jobs.append((lambda vc=vc: jnp.dot(h_kv, wkv_ref[:, vc], preferred_element_type=F32),
                     lambda u, g=g, d=d: emit(u, v_refs[g], d)))
    _pipelined(jobs)


def _qkv(x2, kvn, qn, wq, wkv, cos_t, sin_t, bsz, seq):
    tm = ROW_TILE
    nt = seq // tm
    n_att = seq // ATT_TILE

    def out_arrays(cols):
        shapes, specs = [], []
        for d in ATT_DILATIONS:
            if d == 1:
                shapes.append(jax.ShapeDtypeStruct((bsz * seq, cols), BF16))
                specs.append(pl.BlockSpec((tm, cols), lambda b, s: (b * nt + s, 0)))
            elif d == 4:
                shapes.append(jax.ShapeDtypeStruct((bsz, n_att, 4, QKV_SUB, 4, QKV_M, cols), BF16))
                specs.append(pl.BlockSpec((None, None, 4, None, 4, QKV_M, cols),
                                          lambda b, s: (b, s // QKV_SUB, 0, s % QKV_SUB, 0, 0, 0)))
            else:
                shapes.append(jax.ShapeDtypeStruct((bsz, n_att, d, QKV_SUB, QKV_M, cols), BF16))
                specs.append(pl.BlockSpec((None, None, d, None, QKV_M, cols),
                                          lambda b, s: (b, s // QKV_SUB, 0, s % QKV_SUB, 0, 0)))
        return shapes, specs

    q_shapes, q_specs = out_arrays(ATT_Q_GROUP_DIM)
    k_shapes, k_specs = out_arrays(ATT_KV_GROUP_DIM)
    v_shapes, v_specs = out_arrays(ATT_KV_GROUP_DIM)
    outs = pl.pallas_call(
        _qkv_kernel,
        grid=(bsz, nt),
        in_specs=[
            pl.BlockSpec((tm, D_MODEL), lambda b, s: (b * nt + s, 0)),
            _const_spec((1, D_MODEL)),
            _const_spec((1, D_MODEL)),
            _const_spec(wq.shape),
            _const_spec(wkv.shape),
            pl.BlockSpec((tm, LANES), lambda b, s: (s, 0)),
            pl.BlockSpec((tm, LANES), lambda b, s: (s, 0)),
        ],
        out_specs=q_specs + k_specs + v_specs,
        out_shape=q_shapes + k_shapes + v_shapes,
        scratch_shapes=[pltpu.VMEM((ATT_Q_GROUP_DIM // LANES, tm, LANES), F32)],
        compiler_params=_params(("parallel", "parallel")),
        name="qkv",
    )(x2, kvn, qn, wq, wkv, cos_t, sin_t)
    return [o.reshape(bsz, seq, o.shape[-1]) for o in outs]


ATT_PITCH = ATT_BLOCK + SUBLANES
ATT_UNROLL = ATT_UNITS


def _ext_rows(d):
    return (ATT_UNITS // d + 1) * d * ATT_BLOCK


def _unit_order(g, rho):
    d = ATT_DILATIONS[g]
    if d == 1:
        return (rho % SUBLANES) * ATT_MAX_DIL + rho // SUBLANES
    if d == 4:
        return (rho % QKV_M) * 4 + rho // QKV_M
    return rho


def _state_segments(g, u):
    d = ATT_DILATIONS[g]
    if d == 1:
        return [(pl.multiple_of(c * ATT_PITCH + u * SUBLANES, SUBLANES), c * SUBLANES, SUBLANES)
                for c in range(ATT_MAX_DIL)]
    if d == 4:
        r4, n4 = u // 4, u % 4
        return [(pl.multiple_of((4 * a + r4) * ATT_PITCH + n4 * QKV_M, SUBLANES), a * QKV_M, QKV_M)
                for a in range(4)]
    return [(pl.multiple_of(u * ATT_PITCH, SUBLANES), 0, ATT_BLOCK)]


def _attn_kernel(q0_ref, q1_ref, q2_ref,
                 k0c_ref, k1c_ref, k2c_ref, v0c_ref, v1c_ref, v2c_ref,
                 k0p_ref, k1p_ref, k2p_ref, v0p_ref, v1p_ref, v2p_ref,
                 o_ref, acc_ref, m_ref, l_ref, mask_ref, eye_ref,
                 ke0_ref, ke1_ref, ke2_ref, ve0_ref, ve1_ref, ve2_ref):
    blk = ATT_BLOCK
    no_prev_tile = (pl.program_id(1) == 0).astype(jnp.int32)

    key_row = lax.broadcasted_iota(jnp.int32, (2 * blk, blk), 0)
    qry_row = lax.broadcasted_iota(jnp.int32, (2 * blk, blk), 1)
    in_prev = key_row < blk
    for g in range(len(ATT_DILATIONS)):
        i_s = _unit_order(g, key_row % blk)
        i_q = _unit_order(g, qry_row)
        cur_ok = (~in_prev) & (i_s <= i_q)
        mask_ref[2 * g] = jnp.where((in_prev & (i_s >= i_q)) | cur_ok, 0.0, NEG_BIG).astype(BF16)
        mask_ref[2 * g + 1] = jnp.where(cur_ok, 0.0, NEG_BIG).astype(BF16)
    e_row = lax.broadcasted_iota(jnp.int32, (ATT_REP * blk, blk), 0)
    e_col = lax.broadcasted_iota(jnp.int32, (ATT_REP * blk, blk), 1)
    eye_ref[...] = jnp.where(e_row % blk == e_col, 1.0, 0.0).astype(BF16)

    groups = ((q0_ref, k0c_ref, v0c_ref, k0p_ref, v0p_ref, ke0_ref, ve0_ref),
              (q1_ref, k1c_ref, v1c_ref, k1p_ref, v1p_ref, ke1_ref, ve1_ref),
              (q2_ref, k2c_ref, v2c_ref, k2p_ref, v2p_ref, ke2_ref, ve2_ref))

    for g, d in enumerate(ATT_DILATIONS):
        _, kc_ref, vc_ref, kp_ref, vp_ref, ke_ref, ve_ref = groups[g]
        per_res = ATT_UNITS // d
        ve_ref[:, ATT_HEAD_DIM:] = jnp.ones((_ext_rows(d), ATT_HEAD_DIM), BF16)
        for r in range(d):
            dst = r * (per_res + 1) * blk
            src_prev = (r * per_res + per_res - 1) * blk
            src = r * per_res * blk
            ke_ref[dst:dst + blk, :] = kp_ref[src_prev:src_prev + blk, :]
            ve_ref[dst:dst + blk, 0:ATT_HEAD_DIM] = vp_ref[src_prev:src_prev + blk, :]
            ke_ref[dst + blk:dst + (per_res + 1) * blk, :] = kc_ref[src:src + per_res * blk, :]
            ve_ref[dst + blk:dst + (per_res + 1) * blk, 0:ATT_HEAD_DIM] = vc_ref[src:src + per_res * blk, :]

    def unit(u, g, q_ref, ke_ref, ve_ref):
        d = ATT_DILATIONS[g]
        per_res = ATT_UNITS // d
        r = u // per_res
        n = u % per_res
        r0 = pl.multiple_of(u * blk, blk)
        e0 = pl.multiple_of((r * (per_res + 1) + n) * blk, blk)
        q = jnp.concatenate([q_ref[pl.ds(r0, blk), hd * LANES:(hd + 1) * LANES] for hd in range(ATT_REP)], axis=0)
        q_aug = jnp.concatenate([q, eye_ref[...]], axis=1)
        k_aug = jnp.concatenate([ke_ref[pl.ds(e0, 2 * blk), :],
                                 mask_ref[2 * g + jnp.where(n == 0, no_prev_tile, 0)]], axis=1)
        s = lax.dot_general(q_aug, k_aug, (((1,), (1,)), ((), ())), preferred_element_type=F32)
        m_cur = jnp.max(s, axis=-1, keepdims=True)
        p = jnp.exp2(s - m_cur).astype(BF16)
        pv = jnp.dot(p, ve_ref[pl.ds(e0, 2 * blk), :], preferred_element_type=F32)
        acc = pv[:, :ATT_HEAD_DIM]
        l_new = pv[:, ATT_HEAD_DIM:]
        m_new = jnp.broadcast_to(m_cur, (ATT_REP * blk, LANES))
        segs = _state_segments(g, u)

        def load(ref):
            return jnp.concatenate([ref[hd, pl.ds(s0, n_rows), :] for hd in range(ATT_REP)
                                    for (s0, _, n_rows) in segs], axis=0)

        if g > 0:
            m_old, l_old, a_old = load(m_ref), load(l_ref), load(acc_ref)
            m_c = m_new
            m_new = jnp.maximum(m_old, m_c)
            w_old = jnp.exp2(m_old - m_new)
            w_cur = jnp.exp2(m_c - m_new)
            l_new = w_old * l_old + w_cur * l_new
            acc = w_old * a_old + w_cur * acc
        for hd in range(ATT_REP):
            for (s0, v0, n_rows) in segs:
                rows = slice(hd * blk + v0, hd * blk + v0 + n_rows)
                m_ref[hd, pl.ds(s0, n_rows), :] = m_new[rows]
                l_ref[hd, pl.ds(s0, n_rows), :] = l_new[rows]
                acc_ref[hd, pl.ds(s0, n_rows), :] = acc[rows]

    for g in range(len(ATT_DILATIONS)):
        q_ref, _, _, _, _, ke_ref, ve_ref = groups[g]

        def body(u, carry, g=g, q_ref=q_ref, ke_ref=ke_ref, ve_ref=ve_ref):
            unit(u, g, q_ref, ke_ref, ve_ref)
            return carry

        lax.fori_loop(0, ATT_UNITS, body, 0, unroll=ATT_UNROLL)

    def finish(m, carry):
        p0 = pl.multiple_of(m * ATT_MAX_DIL, ATT_MAX_DIL)
        for hd in range(ATT_REP):
            halves = []
            for j in range(ATT_MAX_DIL // SUBLANES):
                idx = pl.ds(j * SUBLANES * ATT_PITCH + m, SUBLANES, stride=ATT_PITCH)
                halves.append(acc_ref[hd, idx, :] / l_ref[hd, idx, :])
            o_ref[pl.ds(p0, ATT_MAX_DIL), hd * LANES:(hd + 1) * LANES] = jnp.concatenate(halves, axis=0).astype(BF16)
        return carry

    lax.fori_loop(0, ATT_BLOCK, finish, 0, unroll=ATT_UNROLL)


def _attn(qs, ks, vs, bsz, seq):
    n_att = seq // ATT_TILE
    cur = lambda b, i, j: (b, i, j)
    prev = lambda b, i, j: (b, jnp.maximum(i - 1, 0), j)
    q_spec = pl.BlockSpec((None, ATT_TILE, ATT_REP * ATT_HEAD_DIM), cur)
    kv_cur = pl.BlockSpec((None, ATT_TILE, ATT_HEAD_DIM), cur)
    kv_prev = pl.BlockSpec((None, ATT_TILE, ATT_HEAD_DIM), prev)
    width = ATT_REP * ATT_HEAD_DIM
    stat = pltpu.VMEM((ATT_REP, ATT_MAX_DIL * ATT_PITCH, ATT_HEAD_DIM), F32)
    return pl.pallas_call(
        _attn_kernel,
        grid=(bsz, n_att, ATT_KV_HEADS),
        in_specs=[q_spec] * 3 + [kv_cur] * 6 + [kv_prev] * 6,
        out_specs=pl.BlockSpec((None, ATT_TILE, width), cur),
        out_shape=jax.ShapeDtypeStruct((bsz, seq, ATT_HEADS * ATT_HEAD_DIM), BF16),
        scratch_shapes=[stat, stat, stat,
                        pltpu.VMEM((2 * len(ATT_DILATIONS), 2 * ATT_BLOCK, ATT_BLOCK), BF16),
                        pltpu.VMEM((ATT_REP * ATT_BLOCK, ATT_BLOCK), BF16)]
        + [pltpu.VMEM((_ext_rows(d), ATT_HEAD_DIM), BF16) for d in ATT_DILATIONS]
        + [pltpu.VMEM((_ext_rows(d), 2 * ATT_HEAD_DIM), BF16) for d in ATT_DILATIONS],
        compiler_params=_params(("parallel", "parallel", "parallel")),
        name="attn",
    )(*qs, *ks, *vs, *ks, *vs)


def _permute_head_dims(w, n_heads):
    half = ROPE_DIM // 2
    n_low = LANES // 2 - half
    w = w.reshape(w.shape[0], n_heads, ATT_HEAD_DIM)
    w = jnp.concatenate([w[..., :half], w[..., ROPE_DIM:ROPE_DIM + n_low], w[..., half:ROPE_DIM],
                         w[..., ROPE_DIM + n_low:]], axis=-1)
    return w.reshape(w.shape[0], n_heads * ATT_HEAD_DIM)


def _rope_tables(seq):
    half = ROPE_DIM // 2
    inv_freq = jnp.power(jnp.float32(ROPE_THETA), -jnp.arange(0, ROPE_DIM, 2, dtype=F32) / ROPE_DIM)
    ang = jnp.arange(seq, dtype=jnp.int32).astype(F32)[:, None] * inv_freq[None, :]
    cos, sin = jnp.cos(ang), jnp.sin(ang)
    ones = jnp.ones((seq, LANES // 2 - half), F32)
    zeros = jnp.zeros_like(ones)
    cos_t = jnp.concatenate([cos, ones, cos, ones], axis=1)
    sin_t = jnp.concatenate([-sin, zeros, sin, zeros], axis=1)
    regroup = lambda tbl: tbl.reshape(
        seq // ROW_TILE, QKV_M, ATT_MAX_DIL, LANES).transpose(0, 2, 1, 3).reshape(seq, LANES)
    return regroup(cos_t), regroup(sin_t)


def _head_expand():
    head = jnp.arange(LANES)[:, None]
    chan_head = (jnp.arange(SSM_D_INNER) // SSM_HEAD_DIM)[None, :]
    e = (head == chan_head).astype(BF16)
    return jnp.concatenate([e, e], axis=0)


def _dt_weights(w):
    w = jnp.pad(w, ((0, 0), (0, LANES - SSM_N_HEADS)))
    hi = w.astype(BF16)
    lo = (w - hi.astype(F32)).astype(BF16)
    return jnp.concatenate([hi, hi, lo], axis=0)


def kernel(x, a_norm, ssm_w_in, ssm_conv_w, ssm_conv_b, ssm_dt_bias, ssm_a_log, ssm_d, ssm_norm, ssm_w_out,
           kv_norm, w_kv, b_norm, att_w_q, att_w_o, ffn_norm, ffn_w_up, ffn_conv_w, ffn_w_down, final_norm):
    bsz, seq, dm = x.shape
    assert dm == D_MODEL and seq % ATT_TILE == 0
    assert a_norm.shape[0] == 1 and b_norm.shape[0] == 1 and ffn_norm.shape[0] == 2
    t = bsz * seq
    x2 = x.reshape(t, dm)
    row = lambda v: v.reshape(1, -1).astype(F32)

    w_in = ssm_w_in[0]
    wz = w_in[:, :SSM_D_INNER].astype(BF16)
    wx = w_in[:, SSM_D_INNER:SSM_D_INNER + SSM_CONV_DIM].astype(BF16)
    wdt3 = _dt_weights(w_in[:, SSM_D_INNER + SSM_CONV_DIM:])
    pad_heads = lambda v: jnp.pad(row(v), ((0, 0), (0, LANES - SSM_N_HEADS)))
    z, xbc, dt = _in_proj(x2, row(a_norm[0]), wz, wx, wdt3, ssm_conv_w[0], row(ssm_conv_b[0]), bsz, seq)
    x2 = _ssd(xbc, z, dt, x2, pad_heads(ssm_dt_bias[0]), pad_heads(ssm_a_log[0]),
              row(jnp.repeat(ssm_d[0], SSM_HEAD_DIM)), row(ssm_norm[0]), _head_expand(), ssm_w_out[0].astype(BF16),
              bsz, seq)
    x2 = _ffn(x2, None, None, row(ffn_norm[0]), ffn_w_up[0].astype(BF16), ffn_conv_w[0],
              ffn_w_down[0].astype(BF16), row(final_norm), bsz, seq, False)

    cos_t, sin_t = _rope_tables(seq)
    n_q_heads = len(ATT_DILATIONS) * ATT_HEADS
    n_k_heads = len(ATT_DILATIONS) * ATT_KV_HEADS
    k_cols = n_k_heads * ATT_HEAD_DIM
    wq = _permute_head_dims(att_w_q[0], n_q_heads).astype(BF16)
    wkv = jnp.concatenate([_permute_head_dims(w_kv[:, :k_cols], n_k_heads), w_kv[:, k_cols:]], axis=1).astype(BF16)
    outs = _qkv(x2, row(kv_norm), row(b_norm[0]), wq, wkv, cos_t, sin_t, bsz, seq)
    o = _attn(outs[0:3], outs[3:6], outs[6:9], bsz, seq)
    x2 = _ffn(x2, o.reshape(t, dm), att_w_o[0].astype(BF16), row(ffn_norm[1]), ffn_w_up[1].astype(BF16),
              ffn_conv_w[1], ffn_w_down[1].astype(BF16), row(final_norm), bsz, seq, True)
    return x2.reshape(bsz, seq, dm)
```
